```python
import math
import jax, jax.numpy as jnp
from jax import lax
import numpy as np

D_MODEL = 1024
BATCH = 8
SEQ = 2048
DEPTH = 2

HEAD_DIM = 64
H_SB = 6
H_MOBA = 5
H_DSA = 5
W_SB = H_SB * HEAD_DIM
W_MOBA = H_MOBA * HEAD_DIM
W_DSA = H_DSA * HEAD_DIM
MOBA_BLOCK = 256
MOBA_TOPK = 3
MOBA_Q_CHUNK = 64
DSA_TOPK_MAX = 256
IDX_HEADS = 8
IDX_DIM = 64
Q_BLOCK = 128
D_FF = -(-8 * D_MODEL // (3 * 256)) * 256
N_BRANCH = 3
D_IN = 3 * W_SB + 3 * W_MOBA + 3 * W_DSA + IDX_HEADS * IDX_DIM + IDX_DIM + IDX_HEADS + N_BRANCH * D_MODEL
RMS_EPS = 1e-6
NEG_BIG = -1e30
ALIBI_HEADS = H_MOBA + H_DSA

kernel_name = "hybrid_sb_moba_dsa_adaln_sandwich"


def _mixer_split_points():
    sizes = [W_SB] * 3 + [W_MOBA] * 3 + [W_DSA] * 3 + [IDX_HEADS * IDX_DIM, IDX_DIM, IDX_HEADS, N_BRANCH * D_MODEL]
    return [int(v) for v in np.cumsum(sizes)[:-1]]


def _alibi_slopes():
    return jnp.asarray(2.0 ** (-8.0 * np.arange(1, ALIBI_HEADS + 1) / ALIBI_HEADS), dtype=jnp.float32)


def rms_norm(x, g):
    xf = x.astype(jnp.float32)
    y = xf * lax.rsqrt(jnp.mean(xf * xf, axis=-1, keepdims=True) + RMS_EPS)
    return (y * g.astype(jnp.float32)).astype(x.dtype)


def stick_breaking_attention(q, k, v):
    B, S, H, dh = q.shape
    scale = dh ** -0.5
    kpos = jnp.arange(S)

    def block(i):
        start = i * Q_BLOCK
        qb = lax.dynamic_slice_in_dim(q, start, Q_BLOCK, axis=1)
        z = jnp.einsum('bqhd,bshd->bhqs', qb, k, preferred_element_type=jnp.float32) * scale
        qpos = start + jnp.arange(Q_BLOCK)
        past = kpos[None, :] < qpos[:, None]
        log_beta = jax.nn.log_sigmoid(z)
        log_one_minus = jnp.where(past, jax.nn.log_sigmoid(-z), 0.0)
        between = lax.cumsum(log_one_minus, axis=3, reverse=True) - log_one_minus
        a = jnp.where(past, jnp.exp(log_beta + between), 0.0)
        return jnp.einsum('bhqs,bshd->bqhd', a.astype(v.dtype), v)

    out = lax.map(block, jnp.arange(S // Q_BLOCK))
    return jnp.moveaxis(out, 0, 1).reshape(B, S, H * dh)


def moba_attention(q, k, v, slopes):
    B, S, H, dh = q.shape
    scale = dh ** -0.5
    nb = -(-S // MOBA_BLOCK)
    pad = nb * MOBA_BLOCK - S
    kp = jnp.pad(k, ((0, 0), (0, pad), (0, 0), (0, 0)))
    vp = jnp.pad(v, ((0, 0), (0, pad), (0, 0), (0, 0)))
    kb = kp.reshape(B, nb, MOBA_BLOCK, H, dh)
    vb = vp.reshape(B, nb, MOBA_BLOCK, H, dh)
    k_mean = jnp.mean(kb.astype(jnp.float32), axis=2)
    kb_h = jnp.moveaxis(kb, 3, 1)
    vb_h = jnp.moveaxis(vb, 3, 1)
    n_sel = min(MOBA_TOPK, nb - 1)
    b_ix = jnp.arange(B)[:, None, None, None]
    h_ix = jnp.arange(H)[None, :, None, None]
    blk_off = jnp.arange(MOBA_BLOCK)

    def chunk(i):
        start = i * MOBA_Q_CHUNK
        own = start // MOBA_BLOCK
        qc = lax.dynamic_slice_in_dim(q, start, MOBA_Q_CHUNK, axis=1)
        qpos = start + jnp.arange(MOBA_Q_CHUNK)
        k_own = lax.dynamic_slice_in_dim(kp, own * MOBA_BLOCK, MOBA_BLOCK, axis=1)
        v_own = lax.dynamic_slice_in_dim(vp, own * MOBA_BLOCK, MOBA_BLOCK, axis=1)
        own_pos = own * MOBA_BLOCK + blk_off
        dist_own = (qpos[:, None] - own_pos[None, :]).astype(jnp.float32)
        s_own = jnp.einsum('bqhd,bkhd->bhqk', qc, k_own, preferred_element_type=jnp.float32) * scale \
            - slopes[None, :, None, None] * dist_own[None, None]
        s_own = jnp.where((own_pos[None, :] <= qpos[:, None])[None, None], s_own, -jnp.inf)
        if n_sel > 0:
            gate = jnp.einsum('bqhd,bnhd->bhqn', qc.astype(jnp.float32), k_mean)
            gate = jnp.where(jnp.arange(nb) < own, gate, NEG_BIG)
            _, sel = lax.top_k(gate, n_sel)
            valid = sel < own
            k_sel = kb_h[b_ix, h_ix, sel]
            v_sel = vb_h[b_ix, h_ix, sel]
            sel_pos = sel[..., None] * MOBA_BLOCK + blk_off
            dist_sel = (qpos[None, None, :, None, None] - sel_pos).astype(jnp.float32)
            s_sel = jnp.einsum('bqhd,bhqnkd->bhqnk', qc, k_sel, preferred_element_type=jnp.float32) * scale \
                - slopes[None, :, None, None, None] * dist_sel
            s_sel = jnp.where(valid[..., None], s_sel, -jnp.inf).reshape(B, H, MOBA_Q_CHUNK, n_sel * MOBA_BLOCK)
            p = jax.nn.softmax(jnp.concatenate([s_sel, s_own], axis=-1), axis=-1).astype(v.dtype)
            p_sel = p[..., :n_sel * MOBA_BLOCK].reshape(B, H, MOBA_Q_CHUNK, n_sel, MOBA_BLOCK)
            p_own = p[..., n_sel * MOBA_BLOCK:]
            return jnp.einsum('bhqnk,bhqnkd->bqhd', p_sel, v_sel) + jnp.einsum('bhqk,bkhd->bqhd', p_own, v_own)
        p_own = jax.nn.softmax(s_own, axis=-1).astype(v.dtype)
        return jnp.einsum('bhqk,bkhd->bqhd', p_own, v_own)

    out = lax.map(chunk, jnp.arange(S // MOBA_Q_CHUNK))
    return jnp.moveaxis(out, 0, 1).reshape(B, S, H * dh)


def dsa_attention(q, k, v, q_idx, k_idx, w_idx, slopes):
    B, S, H, dh = q.shape
    scale = dh ** -0.5
    top = min(DSA_TOPK_MAX, S // 4)
    kpos = jnp.arange(S)
    b_ix = jnp.arange(B)[:, None, None]

    def block(i):
        start = i * Q_BLOCK
        qi = lax.dynamic_slice_in_dim(q_idx, start, Q_BLOCK, axis=1)
        wi = lax.dynamic_slice_in_dim(w_idx, start, Q_BLOCK, axis=1)
        qc = lax.dynamic_slice_in_dim(q, start, Q_BLOCK, axis=1)
        qpos = start + jnp.arange(Q_BLOCK)
        logits = jnp.einsum('bqhd,bsd->bqhs', qi, k_idx, preferred_element_type=jnp.float32)
        score = jnp.einsum('bqhs,bqh->bqs', jax.nn.relu(logits), wi.astype(jnp.float32))
        admissible = kpos[None, :] <= qpos[:, None]
        score = jnp.where(admissible[None], score, NEG_BIG)
        _, idx = lax.top_k(score, top)
        valid = idx <= qpos[None, :, None]
        k_sel = k[b_ix, idx]
        v_sel = v[b_ix, idx]
        dist = (qpos[None, :, None] - idx).astype(jnp.float32)
        s = jnp.einsum('bqhd,bqkhd->bhqk', qc, k_sel, preferred_element_type=jnp.float32) * scale \
            - slopes[None, :, None, None] * dist[:, None]
        s = jnp.where(valid[:, None], s, -jnp.inf)
        p = jax.nn.softmax(s, axis=-1).astype(v.dtype)
        return jnp.einsum('bhqk,bqkhd->bqhd', p, v_sel)

    out = lax.map(block, jnp.arange(S // Q_BLOCK))
    return jnp.moveaxis(out, 0, 1).reshape(B, S, H * dh)


def hybrid_mixer(h, w_in, b_gate, w_proj_sb, w_proj_moba, w_proj_dsa, w_o, slopes):
    B, S, _ = h.shape
    proj = h @ w_in
    (q_sb, k_sb, v_sb, q_mb, k_mb, v_mb, q_ds, k_ds, v_ds,
     q_ix, k_ix, w_ix, gate_logits) = jnp.split(proj, _mixer_split_points(), axis=-1)
    heads = lambda t: t.reshape(B, S, -1, HEAD_DIM)
    o_sb = stick_breaking_attention(heads(q_sb), heads(k_sb), heads(v_sb))
    o_mb = moba_attention(heads(q_mb), heads(k_mb), heads(v_mb), slopes[0::2])
    o_ds = dsa_attention(heads(q_ds), heads(k_ds), heads(v_ds),
                         q_ix.reshape(B, S, IDX_HEADS, IDX_DIM), k_ix, w_ix, slopes[1::2])
    g_sb, g_mb, g_ds = jnp.split(jax.nn.sigmoid(gate_logits + b_gate), N_BRANCH, axis=-1)
    merged = g_sb * (o_sb @ w_proj_sb) + g_mb * (o_mb @ w_proj_moba) + g_ds * (o_ds @ w_proj_dsa)
    return merged @ w_o


def swiglu(h, w_up, w_down):
    gate, up = jnp.split(h @ w_up, 2, axis=-1)
    return (jax.nn.silu(gate) * up) @ w_down


def setup_inputs(seed: int = 0) -> dict:
    key = jax.random.key(seed)
    ks = jax.random.split(key, 16)
    nrm = lambda k, shape, s: jax.random.normal(k, shape, jnp.float32) * s
    L, D = DEPTH, D_MODEL
    return {
        "x": nrm(ks[0], (BATCH, SEQ, D), 1.0),
        "c": nrm(ks[1], (BATCH, D), 1.0),
        "w_ada": nrm(ks[2], (L, D, 6 * D), 0.5 * D ** -0.5),
        "b_ada": nrm(ks[3], (L, 6 * D), 0.01),
        "g_pre_mix": 1.0 + nrm(ks[4], (L, D), 0.02),
        "g_post_mix": 1.0 + nrm(ks[5], (L, D), 0.02),
        "w_in": nrm(ks[6], (L, D, D_IN), D ** -0.5),
        "b_gate": nrm(ks[7], (L, N_BRANCH * D), 0.01),
        "w_proj_sb": nrm(ks[8], (L, W_SB, D), W_SB ** -0.5),
        "w_proj_moba": nrm(ks[9], (L, W_MOBA, D), W_MOBA ** -0.5),
        "w_proj_dsa": nrm(ks[10], (L, W_DSA, D), W_DSA ** -0.5),
        "w_o": nrm(ks[11], (L, D, D), D ** -0.5),
        "g_pre_ffn": 1.0 + nrm(ks[12], (L, D), 0.02),
        "g_post_ffn": 1.0 + nrm(ks[13], (L, D), 0.02),
        "w_up": nrm(ks[14], (L, D, 2 * D_FF), D ** -0.5),
        "w_down": nrm(ks[15], (L, D_FF, D), D_FF ** -0.5),
    }


def reference(x, c, w_ada, b_ada, g_pre_mix, g_post_mix, w_in, b_gate, w_proj_sb, w_proj_moba,
              w_proj_dsa, w_o, g_pre_ffn, g_post_ffn, w_up, w_down):
    slopes = _alibi_slopes()
    c_act = jax.nn.silu(c)
    for l in range(DEPTH):
        mod = c_act @ w_ada[l] + b_ada[l]
        sh1, sc1, g1, sh2, sc2, g2 = [m[:, None, :] for m in jnp.split(mod, 6, axis=-1)]
        h = rms_norm(x, g_pre_mix[l]) * (1.0 + sc1) + sh1
        y = hybrid_mixer(h, w_in[l], b_gate[l], w_proj_sb[l], w_proj_moba[l], w_proj_dsa[l], w_o[l], slopes)
        x = x + g1 * rms_norm(y, g_post_mix[l])
        h = rms_norm(x, g_pre_ffn[l]) * (1.0 + sc2) + sh2
        y = swiglu(h, w_up[l], w_down[l])
        x = x + g2 * rms_norm(y, g_post_ffn[l])
    return x
```

```python
import functools

import numpy as np
import jax
import jax.numpy as jnp
from jax import lax
from jax.experimental import pallas as pl
from jax.experimental.pallas import tpu as pltpu

F32 = jnp.float32
BF16 = jnp.bfloat16
I32 = jnp.int32

D_MODEL = 1024
HEAD_DIM = 64
H_SB, H_MOBA, H_DSA = 6, 5, 5
W_SB, W_MOBA, W_DSA = H_SB * HEAD_DIM, H_MOBA * HEAD_DIM, H_DSA * HEAD_DIM
MOBA_BLOCK = 256
MOBA_TOPK = 3
DSA_TOPK_MAX = 256
IDX_HEADS = 8
IDX_DIM = 64
D_FF = 2816
N_BRANCH = 3
RMS_EPS = 1e-6
NEG_BIG = -1e30
ALIBI_HEADS = H_MOBA + H_DSA

LANES = 128
PAIR_W = 3 * LANES
QB = 256
KC = 256
INT_MIN = -(2 ** 31)

COL_KX = 9 * PAIR_W
COL_QX = COL_KX + LANES
COL_GATE = COL_QX + IDX_HEADS * IDX_DIM
N_PROJ = COL_GATE + N_BRANCH * D_MODEL
PROJ_TN = 512

NT = (((1,), (1,)), ((), ()))
TN = (((0,), (0,)), ((), ()))

_ALIBI = [float(2.0 ** (-8.0 * h / ALIBI_HEADS)) for h in range(1, ALIBI_HEADS + 1)]
SLOPES_MOBA = _ALIBI[0::2]
SLOPES_DSA = _ALIBI[1::2]


def _rms(x):
    return x * lax.rsqrt(jnp.mean(x * x, axis=-1, keepdims=True) + RMS_EPS)


def _half_mask(shape, lane_axis, hh):
    lane = lax.broadcasted_iota(I32, shape, lane_axis)
    return (lane < HEAD_DIM) if hh == 0 else (lane >= HEAD_DIM)


def _ada_kernel(c_ref, w_ref, b_ref, o_ref):
    c = c_ref[...]
    ca = c * jax.nn.sigmoid(c)
    o_ref[...] = jnp.dot(ca, w_ref[...], preferred_element_type=F32) + b_ref[...]


def _ada(c, w_ada, b_ada):
    depth, d, n = w_ada.shape
    bsz = c.shape[0]
    tn = 512
    return pl.pallas_call(
        _ada_kernel,
        grid=(depth, n // tn),
        in_specs=[
            pl.BlockSpec((bsz, d), lambda l, j: (0, 0)),
            pl.BlockSpec((None, d, tn), lambda l, j: (l, 0, j)),
            pl.BlockSpec((None, 1, tn), lambda l, j: (l, 0, j)),
        ],
        out_specs=pl.BlockSpec((None, bsz, tn), lambda l, j: (l, 0, j)),
        out_shape=jax.ShapeDtypeStruct((depth, bsz, n), F32),
        name="ada_mod",
    )(c, w_ada, b_ada.reshape(depth, 1, n))


def _inproj_kernel(x_ref, mod_ref, g_ref, w_ref, b_ref, o_ref, h_ref, *, n_plain):
    j = pl.program_id(1)

    @pl.when(j == 0)
    def _():
        xh = _rms(x_ref[...])
        h = (xh * g_ref[...]) * (1.0 + mod_ref[1:2, :]) + mod_ref[0:1, :]
        h_ref[...] = h.astype(BF16)

    acc = jnp.dot(h_ref[...], w_ref[...], preferred_element_type=F32) + b_ref[...]

    @pl.when(j < n_plain)
    def _():
        o_ref[...] = acc.astype(BF16)

    @pl.when(j >= n_plain)
    def _():
        o_ref[...] = jax.nn.sigmoid(acc).astype(BF16)


def _inproj(x2d, mod_l, g_pre, w_all, bias, seq):
    t, d = x2d.shape
    tm = 1024
    n = w_all.shape[1]
    per_b = seq // tm
    return pl.pallas_call(
        functools.partial(_inproj_kernel, n_plain=COL_GATE // PROJ_TN),
        grid=(t // tm, n // PROJ_TN),
        in_specs=[
            pl.BlockSpec((tm, d), lambda i, j: (i, 0)),
            pl.BlockSpec((None, 6, d), lambda i, j: (i // per_b, 0, 0)),
            pl.BlockSpec((1, d), lambda i, j: (0, 0)),
            pl.BlockSpec((d, PROJ_TN), lambda i, j: (0, j)),
            pl.BlockSpec((1, PROJ_TN), lambda i, j: (0, j)),
        ],
        out_specs=pl.BlockSpec((tm, PROJ_TN), lambda i, j: (i, j)),
        out_shape=jax.ShapeDtypeStruct((t, n), BF16),
        scratch_shapes=[pltpu.VMEM((tm, d), BF16)],
        compiler_params=pltpu.CompilerParams(
            dimension_semantics=("parallel", "arbitrary"),
            vmem_limit_bytes=48 * 1024 * 1024),
        name="in_proj",
    )(x2d, mod_l, g_pre, w_all, bias)


def _sb_kernel(q_ref, k_ref, v_ref, o_ref):
    qi = pl.program_id(1)
    row = lax.broadcasted_iota(I32, (KC, QB), 0)
    lane = lax.broadcasted_iota(I32, (KC, QB), 1)
    past_diag = row < lane
    upper = (lax.broadcasted_iota(I32, (KC, KC), 1) > lax.broadcasted_iota(I32, (KC, KC), 0)).astype(BF16)
    out_row = lax.broadcasted_iota(I32, (LANES, QB), 0)

    for p in range(PAIR_W // LANES):
        cols = slice(p * LANES, (p + 1) * LANES)
        qp = q_ref[:, cols]
        halves = []
        for hh in range(2):
            qh = jnp.where(_half_mask(qp.shape, 1, hh), qp, jnp.zeros_like(qp))

            def chunk(c, carry, acc, diag, qh=qh, cols=cols):
                start = pl.multiple_of(c * KC, KC)
                kc = k_ref[pl.ds(start, KC), cols]
                vc = v_ref[pl.ds(start, KC), cols]
                z = lax.dot_general(kc, qh, NT, preferred_element_type=F32)
                sp = jnp.log(1.0 + jnp.exp(-jnp.abs(z)))
                log_beta = jnp.minimum(z, 0.0) - sp
                log_1m = -jnp.maximum(z, 0.0) - sp
                if diag:
                    log_1m = jnp.where(past_diag, log_1m, 0.0)
                hi = log_1m.astype(BF16)
                lo = (log_1m - hi.astype(F32)).astype(BF16)
                between = (jnp.dot(upper, hi, preferred_element_type=F32)
                           + jnp.dot(upper, lo, preferred_element_type=F32) + carry)
                a = jnp.exp(log_beta + between)
                if diag:
                    a = jnp.where(past_diag, a, 0.0)
                acc = acc + lax.dot_general(vc, a.astype(BF16), TN, preferred_element_type=F32)
                carry = carry + jnp.sum(log_1m, axis=0, keepdims=True)
                return carry, acc

            carry, acc = chunk(qi, jnp.zeros((1, QB), F32), jnp.zeros((LANES, QB), F32), True)
            carry, acc = lax.fori_loop(
                0, qi, lambda i, ca: chunk(qi - 1 - i, ca[0], ca[1], False), (carry, acc))
            halves.append(acc)
        o_ref[cols, :] = jnp.where(out_row < HEAD_DIM, halves[0], halves[1]).astype(BF16)


def _sb_attention(proj3):
    bsz, seq, _ = proj3.shape
    return pl.pallas_call(
        _sb_kernel,
        grid=(bsz, seq // QB),
        in_specs=[
            pl.BlockSpec((None, QB, PAIR_W), lambda b, i: (b, i, 0)),
            pl.BlockSpec((None, seq, PAIR_W), lambda b, i: (b, 0, 1)),
            pl.BlockSpec((None, seq, PAIR_W), lambda b, i: (b, 0, 2)),
        ],
        out_specs=pl.BlockSpec((None, PAIR_W, QB), lambda b, i: (b, 0, i)),
        out_shape=jax.ShapeDtypeStruct((bsz, PAIR_W, seq), BF16),
        compiler_params=pltpu.CompilerParams(
            dimension_semantics=("parallel", "arbitrary"),
            vmem_limit_bytes=48 * 1024 * 1024),
        name="sb_attn",
    )(proj3, proj3, proj3)


def _softmax_step(kc, vc, qh, bias, m, l, acc):
    s = lax.dot_general(kc, qh, NT, preferred_element_type=F32) + bias
    m_new = jnp.maximum(m, jnp.max(s, axis=0, keepdims=True))
    alpha = jnp.exp(m - m_new)
    p = jnp.exp(s - m_new)
    l = alpha * l + jnp.sum(p, axis=0, keepdims=True)
    acc = alpha * acc + lax.dot_general(vc, p.astype(BF16), TN, preferred_element_type=F32)
    return m_new, l, acc


def _softmax_init():
    return (jnp.full((1, QB), NEG_BIG, F32), jnp.zeros((1, QB), F32), jnp.zeros((LANES, QB), F32))


def _moba_kernel(q_ref, k_ref, v_ref, o_ref, kmean_ref, sel_ref):
    qi = pl.program_id(1)
    n_blocks = kmean_ref.shape[0]

    @pl.when(qi == 0)
    def _():
        for n in range(n_blocks):
            kb = k_ref[n * MOBA_BLOCK:(n + 1) * MOBA_BLOCK, :].astype(F32)
            kmean_ref[n:n + 1, :] = jnp.mean(kb, axis=0, keepdims=True)

    row = lax.broadcasted_iota(I32, (KC, QB), 0)
    lane = lax.broadcasted_iota(I32, (KC, QB), 1)
    causal_diag = row <= lane
    key_minus_query = (row - lane).astype(F32)
    out_row = lax.broadcasted_iota(I32, (LANES, QB), 0)
    blk = lax.broadcasted_iota(I32, (n_blocks, QB), 0)

    for p in range(PAIR_W // LANES):
        cols = slice(p * LANES, (p + 1) * LANES)
        qp = q_ref[:, cols]
        halves = []
        for hh in range(2):
            head = 2 * p + hh
            if head >= H_MOBA:
                continue
            slope = SLOPES_MOBA[head]
            qh = jnp.where(_half_mask(qp.shape, 1, hh), qp, jnp.zeros_like(qp))

            gate = lax.dot_general(kmean_ref[:, cols], qh.astype(F32), NT,
                                   preferred_element_type=F32, precision=lax.Precision.HIGHEST)
            rank = jnp.zeros((n_blocks, QB), I32)
            for mb in range(n_blocks):
                gm = gate[mb:mb + 1, :]
                beats = (gm > gate) | ((gm == gate) & (mb < blk))
                rank = rank + jnp.where(beats, (mb < qi).astype(I32), 0)
            selected = (rank < MOBA_TOPK) & (blk < qi)
            sel_ref[...] = jnp.where(selected, 0.0, NEG_BIG)

            def chunk(c, state, diag, qh=qh, cols=cols, slope=slope):
                start = pl.multiple_of(c * KC, KC)
                kc = k_ref[pl.ds(start, KC), cols]
                vc = v_ref[pl.ds(start, KC), cols]
                if diag:
                    bias = jnp.where(causal_diag, slope * key_minus_query, NEG_BIG)
                else:
                    offset = ((c - qi) * KC).astype(F32)
                    bias = slope * (key_minus_query + offset) + sel_ref[pl.ds(c, 1), :]
                return _softmax_step(kc, vc, qh, bias, *state)

            state = chunk(qi, _softmax_init(), True)
            m, l, acc = lax.fori_loop(0, qi, lambda c, st: chunk(c, st, False), state)
            halves.append(acc / l)
        if len(halves) == 2:
            out = jnp.where(out_row < HEAD_DIM, halves[0], halves[1])
        else:
            out = halves[0]
        o_ref[cols, :] = out.astype(BF16)


def _moba_attention(proj3):
    bsz, seq, _ = proj3.shape
    return pl.pallas_call(
        _moba_kernel,
        grid=(bsz, seq // QB),
        in_specs=[
            pl.BlockSpec((None, QB, PAIR_W), lambda b, i: (b, i, 3)),
            pl.BlockSpec((None, seq, PAIR_W), lambda b, i: (b, 0, 4)),
            pl.BlockSpec((None, seq, PAIR_W), lambda b, i: (b, 0, 5)),
        ],
        out_specs=pl.BlockSpec((None, PAIR_W, QB), lambda b, i: (b, 0, i)),
        out_shape=jax.ShapeDtypeStruct((bsz, PAIR_W, seq), BF16),
        scratch_shapes=[pltpu.VMEM((seq // MOBA_BLOCK, PAIR_W), F32),
                        pltpu.VMEM((seq // MOBA_BLOCK, QB), F32)],
        compiler_params=pltpu.CompilerParams(
            dimension_semantics=("parallel", "arbitrary"),
            vmem_limit_bytes=48 * 1024 * 1024),
        name="moba_attn",
    )(proj3, proj3, proj3)


def _dsa_kernel(q_ref, k_ref, v_ref, qx_ref, kxq_ref, kx_ref, o_ref,
                kx2_ref, key_ref, mb_ref, cidx_ref, *, top):
    qi = pl.program_id(1)
    seq = k_ref.shape[0]

    @pl.when(qi == 0)
    def _():
        kx = kx_ref[...].astype(F32)
        dup = jnp.where(_half_mask(kx.shape, 1, 0), kx, pltpu.roll(kx, HEAD_DIM, 1))
        kx2_ref[...] = dup.astype(BF16)

    row = lax.broadcasted_iota(I32, (KC, QB), 0)
    lane = lax.broadcasted_iota(I32, (KC, QB), 1)
    causal_diag = row <= lane
    key_minus_query = (row - lane).astype(F32)
    out_row = lax.broadcasted_iota(I32, (LANES, QB), 0)

    pick = (lax.broadcasted_iota(I32, (IDX_HEADS, LANES), 1)
            == lax.broadcasted_iota(I32, (IDX_HEADS, LANES), 0) + IDX_DIM).astype(BF16)
    w_t = lax.dot_general(pick, kxq_ref[...], NT, preferred_element_type=F32)

    def score_chunk(c, diag):
        start = pl.multiple_of(c * KC, KC)
        kc = kx2_ref[pl.ds(start, KC), :]
        sc = jnp.zeros((KC, QB), F32)
        for pp in range(IDX_HEADS // 2):
            qp = qx_ref[:, pp * LANES:(pp + 1) * LANES]
            for hh in range(2):
                h = 2 * pp + hh
                qh = jnp.where(_half_mask(qp.shape, 1, hh), qp, jnp.zeros_like(qp))
                lg = lax.dot_general(kc, qh, NT, preferred_element_type=F32)
                sc = sc + jnp.maximum(lg, 0.0) * w_t[h:h + 1, :]
        if diag:
            sc = jnp.where(causal_diag, sc, NEG_BIG)
        bits = pltpu.bitcast(sc, I32)
        key_ref[pl.ds(start, KC), :] = jnp.where(bits < 0, bits ^ jnp.int32(0x7FFFFFFF), bits)

    score_chunk(qi, True)

    def _score_body(c, carry):
        score_chunk(c, False)
        return carry

    lax.fori_loop(0, qi, _score_body, 0)

    def count(pred):
        def body(c, acc):
            start = pl.multiple_of(c * KC, KC)
            hit = pred(key_ref[pl.ds(start, KC), :], c)
            return acc + jnp.sum(jnp.where(hit, 1, 0), axis=0, keepdims=True)
        return lax.fori_loop(0, qi + 1, body, jnp.zeros((1, QB), I32))

    def bit_step(i, tau):
        cand = tau + lax.shift_left(jnp.int32(1), 31 - i)
        cnt = count(lambda keys, c: keys >= cand)
        return jnp.where(cnt >= top, cand, tau)

    n_bits = jnp.where(qi > 0, 32, 0)
    tau = lax.fori_loop(0, n_bits, bit_step, jnp.full((1, QB), INT_MIN, I32))

    cnt_ge = count(lambda keys, c: keys >= tau)
    cnt_gt = count(lambda keys, c: keys > tau)
    need = top - cnt_gt
    cidx_ref[...] = jnp.full(cidx_ref.shape, seq, I32)

    @pl.when(jnp.max(cnt_ge) > top)
    def _():
        def idx_step(i, x):
            cand = x + lax.shift_left(jnp.int32(1), 11 - i)
            cnt = count(lambda keys, c: (keys == tau) & ((c * KC + row) < cand))
            return jnp.where(cnt < need, cand, x)
        x = lax.fori_loop(0, 12, idx_step, jnp.zeros((1, QB), I32))
        cidx_ref[...] = jnp.broadcast_to(x, cidx_ref.shape)

    cidx = cidx_ref[0:1, :]

    def mask_chunk(c, diag):
        start = pl.multiple_of(c * KC, KC)
        keys = key_ref[pl.ds(start, KC), :]
        keep = (keys > tau) | ((keys == tau) & ((c * KC + row) <= cidx))
        if diag:
            keep = keep & causal_diag
        mb_ref[pl.ds(start, KC), :] = jnp.where(keep, 0.0, NEG_BIG)

    mask_chunk(qi, True)

    def _mask_body(c, carry):
        mask_chunk(c, False)
        return carry

    lax.fori_loop(0, qi, _mask_body, 0)

    for p in range(PAIR_W // LANES):
        cols = slice(p * LANES, (p + 1) * LANES)
        qp = q_ref[:, cols]
        halves = []
        for hh in range(2):
            head = 2 * p + hh
            if head >= H_DSA:
                continue
            slope = SLOPES_DSA[head]
            qh = jnp.where(_half_mask(qp.shape, 1, hh), qp, jnp.zeros_like(qp))

            def chunk(c, state, qh=qh, cols=cols, slope=slope):
                start = pl.multiple_of(c * KC, KC)
                kc = k_ref[pl.ds(start, KC), cols]
                vc = v_ref[pl.ds(start, KC), cols]
                offset = ((c - qi) * KC).astype(F32)
                bias = slope * (key_minus_query + offset) + mb_ref[pl.ds(start, KC), :]
                return _softmax_step(kc, vc, qh, bias, *state)

            m, l, acc = lax.fori_loop(0, qi + 1, chunk, _softmax_init())
            halves.append(acc / l)
        if len(halves) == 2:
            out = jnp.where(out_row < HEAD_DIM, halves[0], halves[1])
        else:
            out = halves[0]
        o_ref[cols, :] = out.astype(BF16)


def _dsa_attention(proj3):
    bsz, seq, _ = proj3.shape
    top = min(DSA_TOPK_MAX, seq // 4)
    assert top == QB, "the first query block must keep every admissible key"
    return pl.pallas_call(
        functools.partial(_dsa_kernel, top=top),
        grid=(bsz, seq // QB),
        in_specs=[
            pl.BlockSpec((None, QB, PAIR_W), lambda b, i: (b, i, 6)),
            pl.BlockSpec((None, seq, PAIR_W), lambda b, i: (b, 0, 7)),
            pl.BlockSpec((None, seq, PAIR_W), lambda b, i: (b, 0, 8)),
            pl.BlockSpec((None, QB, IDX_HEADS * IDX_DIM), lambda b, i: (b, i, COL_QX // (IDX_HEADS * IDX_DIM))),
            pl.BlockSpec((None, QB, LANES), lambda b, i: (b, i, COL_KX // LANES)),
            pl.BlockSpec((None, seq, LANES), lambda b, i: (b, 0, COL_KX // LANES)),
        ],
        out_specs=pl.BlockSpec((None, PAIR_W, QB), lambda b, i: (b, 0, i)),
        out_shape=jax.ShapeDtypeStruct((bsz, PAIR_W, seq), BF16),
        scratch_shapes=[pltpu.VMEM((seq, LANES), BF16),
                        pltpu.VMEM((seq, QB), I32),
                        pltpu.VMEM((seq, QB), F32),
                        pltpu.VMEM((8, QB), I32)],
        compiler_params=pltpu.CompilerParams(
            dimension_semantics=("parallel", "arbitrary"),
            vmem_limit_bytes=48 * 1024 * 1024),
        name="dsa_attn",
    )(proj3, proj3, proj3, proj3, proj3, proj3)


def _merge_kernel(osb_ref, omb_ref, ods_ref, gsb_ref, gmb_ref, gds_ref, x_ref, mod_ref, gpost_ref,
                  wsb_ref, wmb_ref, wds_ref, wo_ref, out_ref):
    def branch(o_ref, w_ref, g_ref):
        y = lax.dot_general(o_ref[...], w_ref[...], TN, preferred_element_type=F32)
        return g_ref[...].astype(F32) * y

    merged = branch(osb_ref, wsb_ref, gsb_ref) + branch(omb_ref, wmb_ref, gmb_ref) + branch(ods_ref, wds_ref, gds_ref)
    y = jnp.dot(merged.astype(BF16), wo_ref[...], preferred_element_type=F32)
    out_ref[...] = x_ref[...] + mod_ref[2:3, :] * (_rms(y) * gpost_ref[...])


def _merge(o_sb, o_mb, o_ds, proj, x2d, mod_l, g_post, w_sb, w_mb, w_ds, w_o, seq):
    t, d = x2d.shape
    tm = 512
    per_b = seq // tm
    gate_blk = COL_GATE // d
    o_spec = pl.BlockSpec((None, PAIR_W, tm), lambda i: (i // per_b, 0, i % per_b))
    w_spec = pl.BlockSpec((PAIR_W, d), lambda i: (0, 0))
    return pl.pallas_call(
        _merge_kernel,
        grid=(t // tm,),
        in_specs=[
            o_spec, o_spec, o_spec,
            pl.BlockSpec((tm, d), lambda i: (i, gate_blk)),
            pl.BlockSpec((tm, d), lambda i: (i, gate_blk + 1)),
            pl.BlockSpec((tm, d), lambda i: (i, gate_blk + 2)),
            pl.BlockSpec((tm, d), lambda i: (i, 0)),
            pl.BlockSpec((None, 6, d), lambda i: (i // per_b, 0, 0)),
            pl.BlockSpec((1, d), lambda i: (0, 0)),
            w_spec, w_spec, w_spec,
            pl.BlockSpec((d, d), lambda i: (0, 0)),
        ],
        out_specs=pl.BlockSpec((tm, d), lambda i: (i, 0)),
        out_shape=jax.ShapeDtypeStruct((t, d), F32),
        compiler_params=pltpu.CompilerParams(
            dimension_semantics=("parallel",),
            vmem_limit_bytes=48 * 1024 * 1024),
        name="merge_out",
    )(o_sb, o_mb, o_ds, proj, proj, proj, x2d, mod_l, g_post, w_sb, w_mb, w_ds, w_o)


def _ffn_kernel(x_ref, mod_ref, gpre_ref, gpost_ref, wg_ref, wu_ref, wd_ref, out_ref, h_ref, acc_ref):
    j = pl.program_id(1)

    @pl.when(j == 0)
    def _():
        xh = _rms(x_ref[...])
        h = (xh * gpre_ref[...]) * (1.0 + mod_ref[4:5, :]) + mod_ref[3:4, :]
        h_ref[...] = h.astype(BF16)
        acc_ref[...] = jnp.zeros_like(acc_ref)

    h = h_ref[...]
    gate = jnp.dot(h, wg_ref[...], preferred_element_type=F32)
    up = jnp.dot(h, wu_ref[...], preferred_element_type=F32)
    act = (gate * jax.nn.sigmoid(gate)) * up
    acc_ref[...] += jnp.dot(act.astype(BF16), wd_ref[...], preferred_element_type=F32)

    @pl.when(j == pl.num_programs(1) - 1)
    def _():
        y = acc_ref[...]
        out_ref[...] = x_ref[...] + mod_ref[5:6, :] * (_rms(y) * gpost_ref[...])


def _ffn(x2d, mod_l, g_pre, g_post, w_up, w_down, seq):
    t, d = x2d.shape
    tm = 512
    tf = D_FF // 2
    n_f = D_FF // tf
    per_b = seq // tm
    return pl.pallas_call(
        _ffn_kernel,
        grid=(t // tm, n_f),
        in_specs=[
            pl.BlockSpec((tm, d), lambda i, j: (i, 0)),
            pl.BlockSpec((None, 6, d), lambda i, j: (i // per_b, 0, 0)),
            pl.BlockSpec((1, d), lambda i, j: (0, 0)),
            pl.BlockSpec((1, d), lambda i, j: (0, 0)),
            pl.BlockSpec((d, tf), lambda i, j: (0, j)),
            pl.BlockSpec((d, tf), lambda i, j: (0, j + n_f)),
            pl.BlockSpec((tf, d), lambda i, j: (j, 0)),
        ],
        out_specs=pl.BlockSpec((tm, d), lambda i, j: (i, 0)),
        out_shape=jax.ShapeDtypeStruct((t, d), F32),
        scratch_shapes=[pltpu.VMEM((tm, d), BF16), pltpu.VMEM((tm, d), F32)],
        compiler_params=pltpu.CompilerParams(
            dimension_semantics=("parallel", "arbitrary"),
            vmem_limit_bytes=56 * 1024 * 1024),
        name="ffn",
    )(x2d, mod_l, g_pre, g_post, w_up, w_up, w_down)


def _pad_cols(w, n):
    return jnp.pad(w, ((0, 0), (0, n - w.shape[1])))


def _pad_rows(w, n):
    return jnp.pad(w, ((0, n - w.shape[0]), (0, 0)))


def _layout_w_in(w_in_l, b_gate_l):
    sizes = [W_SB] * 3 + [W_MOBA] * 3 + [W_DSA] * 3 + [IDX_HEADS * IDX_DIM, IDX_DIM, IDX_HEADS, N_BRANCH * D_MODEL]
    points = [int(v) for v in np.cumsum(sizes)[:-1]]
    (q_sb, k_sb, v_sb, q_mb, k_mb, v_mb, q_ds, k_ds, v_ds, q_ix, k_ix, w_ix, gates) = jnp.split(w_in_l, points, axis=1)
    scale = HEAD_DIM ** -0.5
    cols = [q_sb * scale, k_sb, v_sb,
            _pad_cols(q_mb * scale, PAIR_W), _pad_cols(k_mb, PAIR_W), _pad_cols(v_mb, PAIR_W),
            _pad_cols(q_ds * scale, PAIR_W), _pad_cols(k_ds, PAIR_W), _pad_cols(v_ds, PAIR_W),
            _pad_cols(jnp.concatenate([k_ix, w_ix], axis=1), LANES), q_ix, gates]
    w_all = jnp.concatenate(cols, axis=1).astype(BF16)
    bias = jnp.concatenate([jnp.zeros((COL_GATE,), F32), b_gate_l])[None, :]
    return w_all, bias


def kernel(x, c, w_ada, b_ada, g_pre_mix, g_post_mix, w_in, b_gate, w_proj_sb, w_proj_moba,
           w_proj_dsa, w_o, g_pre_ffn, g_post_ffn, w_up, w_down):
    bsz, seq, d = x.shape
    depth = w_ada.shape[0]
    assert d == D_MODEL and seq % QB == 0 and QB == MOBA_BLOCK and KC == QB
    mod = _ada(c, w_ada, b_ada).reshape(depth, bsz, 6, d)
    x2d = x.reshape(bsz * seq, d)
    for l in range(depth):
        w_all, bias = _layout_w_in(w_in[l], b_gate[l])
        assert w_all.shape[1] == N_PROJ
        proj = _inproj(x2d, mod[l], g_pre_mix[l][None, :], w_all, bias, seq)
        proj3 = proj.reshape(bsz, seq, N_PROJ)
        o_sb = _sb_attention(proj3)
        o_mb = _moba_attention(proj3)
        o_ds = _dsa_attention(proj3)
        x2d = _merge(o_sb, o_mb, o_ds, proj, x2d, mod[l], g_post_mix[l][None, :],
                     w_proj_sb[l].astype(BF16),
                     _pad_rows(w_proj_moba[l], PAIR_W).astype(BF16),
                     _pad_rows(w_proj_dsa[l], PAIR_W).astype(BF16),
                     w_o[l].astype(BF16), seq)
        x2d = _ffn(x2d, mod[l], g_pre_ffn[l][None, :], g_post_ffn[l][None, :],
                   w_up[l].astype(BF16), w_down[l].astype(BF16), seq)
    return x2d.reshape(bsz, seq, d)
```

```python
import functools

import numpy as np
import jax
import jax.numpy as jnp
from jax import lax
from jax.experimental import pallas as pl
from jax.experimental.pallas import tpu as pltpu

F32 = jnp.float32
BF16 = jnp.bfloat16
I32 = jnp.int32

D_MODEL = 1024
HEAD_DIM = 64
H_SB, H_MOBA, H_DSA = 6, 5, 5
W_SB, W_MOBA, W_DSA = H_SB * HEAD_DIM, H_MOBA * HEAD_DIM, H_DSA * HEAD_DIM
MOBA_BLOCK = 256
MOBA_TOPK = 3
DSA_TOPK_MAX = 256
IDX_HEADS = 8
IDX_DIM = 64
D_FF = 2816
N_BRANCH = 3
RMS_EPS = 1e-6
NEG_BIG = -1e30
ALIBI_HEADS = H_MOBA + H_DSA

LANES = 128
PAIR_W = 3 * LANES
QB = 256
KC = 256
INT_MIN = -(2 ** 31)

COL_KX = 9 * PAIR_W
COL_QX = COL_KX + LANES
COL_GATE = COL_QX + IDX_HEADS * IDX_DIM
N_PROJ = COL_GATE + N_BRANCH * D_MODEL
PROJ_TN = 512

NT = (((1,), (1,)), ((), ()))
TN = (((0,), (0,)), ((), ()))

_ALIBI = [float(2.0 ** (-8.0 * h / ALIBI_HEADS)) for h in range(1, ALIBI_HEADS + 1)]
SLOPES_MOBA = _ALIBI[0::2]
SLOPES_DSA = _ALIBI[1::2]


def _rms(x):
    return x * lax.rsqrt(jnp.mean(x * x, axis=-1, keepdims=True) + RMS_EPS)


def _half_mask(shape, lane_axis, hh):
    lane = lax.broadcasted_iota(I32, shape, lane_axis)
    return (lane < HEAD_DIM) if hh == 0 else (lane >= HEAD_DIM)


def _ada_kernel(c_ref, w_ref, b_ref, o_ref):
    c = c_ref[...]
    ca = c * jax.nn.sigmoid(c)
    o_ref[...] = jnp.dot(ca, w_ref[...], preferred_element_type=F32) + b_ref[...]


def _ada(c, w_ada, b_ada):
    depth, d, n = w_ada.shape
    bsz = c.shape[0]
    tn = 512
    return pl.pallas_call(
        _ada_kernel,
        grid=(depth, n // tn),
        in_specs=[
            pl.BlockSpec((bsz, d), lambda l, j: (0, 0)),
            pl.BlockSpec((None, d, tn), lambda l, j: (l, 0, j)),
            pl.BlockSpec((None, 1, tn), lambda l, j: (l, 0, j)),
        ],
        out_specs=pl.BlockSpec((None, bsz, tn), lambda l, j: (l, 0, j)),
        out_shape=jax.ShapeDtypeStruct((depth, bsz, n), F32),
        name="ada_mod",
    )(c, w_ada, b_ada.reshape(depth, 1, n))


def _inproj_kernel(x_ref, mod_ref, g_ref, w_ref, b_ref, o_ref, h_ref, *, n_plain):
    j = pl.program_id(1)

    @pl.when(j == 0)
    def _():
        xh = _rms(x_ref[...])
        h = (xh * g_ref[...]) * (1.0 + mod_ref[1:2, :]) + mod_ref[0:1, :]
        h_ref[...] = h.astype(BF16)

    acc = jnp.dot(h_ref[...], w_ref[...], preferred_element_type=F32) + b_ref[...]

    @pl.when(j < n_plain)
    def _():
        o_ref[...] = acc.astype(BF16)

    @pl.when(j >= n_plain)
    def _():
        o_ref[...] = jax.nn.sigmoid(acc).astype(BF16)


def _inproj(x2d, mod_l, g_pre, w_all, bias, seq):
    t, d = x2d.shape
    tm = 1024
    n = w_all.shape[1]
    per_b = seq // tm
    return pl.pallas_call(
        functools.partial(_inproj_kernel, n_plain=COL_GATE // PROJ_TN),
        grid=(t // tm, n // PROJ_TN),
        in_specs=[
            pl.BlockSpec((tm, d), lambda i, j: (i, 0)),
            pl.BlockSpec((None, 6, d), lambda i, j: (i // per_b, 0, 0)),
            pl.BlockSpec((1, d), lambda i, j: (0, 0)),
            pl.BlockSpec((d, PROJ_TN), lambda i, j: (0, j)),
            pl.BlockSpec((1, PROJ_TN), lambda i, j: (0, j)),
        ],
        out_specs=pl.BlockSpec((tm, PROJ_TN), lambda i, j: (i, j)),
        out_shape=jax.ShapeDtypeStruct((t, n), BF16),
        scratch_shapes=[pltpu.VMEM((tm, d), BF16)],
        compiler_params=pltpu.CompilerParams(
            dimension_semantics=("parallel", "arbitrary"),
            vmem_limit_bytes=48 * 1024 * 1024),
        name="in_proj",
    )(x2d, mod_l, g_pre, w_all, bias)


def _masked_heads(ref, dst_ref, n_heads):
    for head in range(n_heads):
        grp = ref[:, (head // 2) * LANES:(head // 2 + 1) * LANES]
        dst_ref[head] = jnp.where(_half_mask(grp.shape, 1, head % 2), grp, jnp.zeros_like(grp))


def _sb_kernel(q_ref, k_ref, v_ref, o_ref, qh_ref, acc_ref):
    qi = pl.program_id(1)
    row = lax.broadcasted_iota(I32, (KC, QB), 0)
    lane = lax.broadcasted_iota(I32, (KC, QB), 1)
    past_diag = row < lane
    upper = (lax.broadcasted_iota(I32, (KC, KC), 1) > lax.broadcasted_iota(I32, (KC, KC), 0)).astype(BF16)

    _masked_heads(q_ref, qh_ref, H_SB)
    acc_ref[...] = jnp.zeros_like(acc_ref)

    def chunk(c, carries, diag):
        start = pl.multiple_of(c * KC, KC)
        new_carries = []
        for p in range(H_SB // 2):
            cols = slice(p * LANES, (p + 1) * LANES)
            kc = k_ref[pl.ds(start, KC), cols]
            vc = v_ref[pl.ds(start, KC), cols]
            pv = None
            for hh in range(2):
                head = 2 * p + hh
                z = lax.dot_general(kc, qh_ref[head], NT, preferred_element_type=F32)
                sp = jnp.log(1.0 + jnp.exp(-jnp.abs(z)))
                log_beta = jnp.minimum(z, 0.0) - sp
                log_1m = -jnp.maximum(z, 0.0) - sp
                if diag:
                    log_1m = jnp.where(past_diag, log_1m, 0.0)
                hi = log_1m.astype(BF16)
                lo = (log_1m - hi.astype(F32)).astype(BF16)
                between = (jnp.dot(upper, hi, preferred_element_type=F32)
                           + jnp.dot(upper, lo, preferred_element_type=F32) + carries[head])
                a = jnp.exp(log_beta + between)
                if diag:
                    a = jnp.where(past_diag, a, 0.0)
                vh = jnp.where(_half_mask(vc.shape, 1, hh), vc, jnp.zeros_like(vc))
                part = lax.dot_general(vh, a.astype(BF16), TN, preferred_element_type=F32)
                pv = part if pv is None else pv + part
                new_carries.append(carries[head] + jnp.sum(log_1m, axis=0, keepdims=True))
            acc_ref[p] += pv
        return tuple(new_carries)

    carries = chunk(qi, tuple(jnp.zeros((1, QB), F32) for _ in range(H_SB)), True)
    lax.fori_loop(0, qi, lambda i, cs: chunk(qi - 1 - i, cs, False), carries)
    for p in range(H_SB // 2):
        o_ref[p * LANES:(p + 1) * LANES, :] = acc_ref[p].astype(BF16)


def _sb_attention(proj3):
    bsz, seq, _ = proj3.shape
    return pl.pallas_call(
        _sb_kernel,
        grid=(bsz, seq // QB),
        in_specs=[
            pl.BlockSpec((None, QB, PAIR_W), lambda b, i: (b, i, 0)),
            pl.BlockSpec((None, seq, PAIR_W), lambda b, i: (b, 0, 1)),
            pl.BlockSpec((None, seq, PAIR_W), lambda b, i: (b, 0, 2)),
        ],
        out_specs=pl.BlockSpec((None, PAIR_W, QB), lambda b, i: (b, 0, i)),
        out_shape=jax.ShapeDtypeStruct((bsz, PAIR_W, seq), BF16),
        scratch_shapes=[pltpu.VMEM((H_SB, QB, LANES), BF16),
                        pltpu.VMEM((H_SB // 2, LANES, QB), F32)],
        compiler_params=pltpu.CompilerParams(
            dimension_semantics=("parallel", "arbitrary"),
            vmem_limit_bytes=48 * 1024 * 1024),
        name="sb_attn",
    )(proj3, proj3, proj3)


def _softmax_heads(qh_ref, k_ref, v_ref, acc_ref, o_ref, qi, n_heads, chunk_bias):
    out_row = lax.broadcasted_iota(I32, (LANES, QB), 0)
    n_pairs = (n_heads + 1) // 2
    acc_ref[...] = jnp.zeros_like(acc_ref)

    def by_head_rows(vals):
        return vals[0] if len(vals) == 1 else jnp.where(out_row < HEAD_DIM, vals[0], vals[1])

    def chunk(c, state, diag):
        ms, ls = state
        start = pl.multiple_of(c * KC, KC)
        bias_of = chunk_bias(c, diag)
        new_ms, new_ls = [], []
        for p in range(n_pairs):
            cols = slice(p * LANES, (p + 1) * LANES)
            kc = k_ref[pl.ds(start, KC), cols]
            vc = v_ref[pl.ds(start, KC), cols]
            heads = [h for h in (2 * p, 2 * p + 1) if h < n_heads]
            pv, alphas = None, []
            for head in heads:
                s = lax.dot_general(kc, qh_ref[head], NT, preferred_element_type=F32) + bias_of(head)
                m_new = jnp.maximum(ms[head], jnp.max(s, axis=0, keepdims=True))
                alpha = jnp.exp(ms[head] - m_new)
                pr = jnp.exp(s - m_new)
                new_ms.append(m_new)
                new_ls.append(alpha * ls[head] + jnp.sum(pr, axis=0, keepdims=True))
                alphas.append(alpha)
                vh = vc if len(heads) == 1 else jnp.where(_half_mask(vc.shape, 1, head % 2), vc, jnp.zeros_like(vc))
                part = lax.dot_general(vh, pr.astype(BF16), TN, preferred_element_type=F32)
                pv = part if pv is None else pv + part
            acc_ref[p] = acc_ref[p] * by_head_rows(alphas) + pv
        return tuple(new_ms), tuple(new_ls)

    init = (tuple(jnp.full((1, QB), NEG_BIG, F32) for _ in range(n_heads)),
            tuple(jnp.zeros((1, QB), F32) for _ in range(n_heads)))
    state = chunk(qi, init, True)
    _, ls = lax.fori_loop(0, qi, lambda c, st: chunk(c, st, False), state)
    for p in range(n_pairs):
        denom = by_head_rows([ls[h] for h in (2 * p, 2 * p + 1) if h < n_heads])
        o_ref[p * LANES:(p + 1) * LANES, :] = (acc_ref[p] / denom).astype(BF16)


def _moba_kernel(q_ref, k_ref, v_ref, o_ref, kmean_ref, sel_ref, qh_ref, acc_ref):
    qi = pl.program_id(1)
    n_blocks = kmean_ref.shape[0]

    @pl.when(qi == 0)
    def _():
        for n in range(n_blocks):
            kb = k_ref[n * MOBA_BLOCK:(n + 1) * MOBA_BLOCK, :].astype(F32)
            kmean_ref[n:n + 1, :] = jnp.mean(kb, axis=0, keepdims=True)

    row = lax.broadcasted_iota(I32, (KC, QB), 0)
    lane = lax.broadcasted_iota(I32, (KC, QB), 1)
    causal_diag = row <= lane
    key_minus_query = (row - lane).astype(F32)
    blk = lax.broadcasted_iota(I32, (n_blocks, QB), 0)

    _masked_heads(q_ref, qh_ref, H_MOBA)

    for head in range(H_MOBA):
        cols = slice((head // 2) * LANES, (head // 2 + 1) * LANES)
        gate = lax.dot_general(kmean_ref[:, cols], qh_ref[head].astype(F32), NT,
                               preferred_element_type=F32, precision=lax.Precision.HIGHEST)
        rank = jnp.zeros((n_blocks, QB), I32)
        for mb in range(n_blocks):
            gm = gate[mb:mb + 1, :]
            beats = (gm > gate) | ((gm == gate) & (mb < blk))
            rank = rank + jnp.where(beats, (mb < qi).astype(I32), 0)
        selected = (rank < MOBA_TOPK) & (blk < qi)
        sel_ref[head] = jnp.where(selected, 0.0, NEG_BIG)

    def chunk_bias(c, diag):
        if diag:
            return lambda head: jnp.where(causal_diag, SLOPES_MOBA[head] * key_minus_query, NEG_BIG)
        dist = key_minus_query + ((c - qi) * KC).astype(F32)
        return lambda head: SLOPES_MOBA[head] * dist + sel_ref[head, pl.ds(c, 1), :]

    _softmax_heads(qh_ref, k_ref, v_ref, acc_ref, o_ref, qi, H_MOBA, chunk_bias)


def _moba_attention(proj3):
    bsz, seq, _ = proj3.shape
    n_blocks = seq // MOBA_BLOCK
    return pl.pallas_call(
        _moba_kernel,
        grid=(bsz, seq // QB),
        in_specs=[
            pl.BlockSpec((None, QB, PAIR_W), lambda b, i: (b, i, 3)),
            pl.BlockSpec((None, seq, PAIR_W), lambda b, i: (b, 0, 4)),
            pl.BlockSpec((None, seq, PAIR_W), lambda b, i: (b, 0, 5)),
        ],
        out_specs=pl.BlockSpec((None, PAIR_W, QB), lambda b, i: (b, 0, i)),
        out_shape=jax.ShapeDtypeStruct((bsz, PAIR_W, seq), BF16),
        scratch_shapes=[pltpu.VMEM((n_blocks, PAIR_W), F32),
                        pltpu.VMEM((H_MOBA, n_blocks, QB), F32),
                        pltpu.VMEM((H_MOBA, QB, LANES), BF16),
                        pltpu.VMEM((PAIR_W // LANES, LANES, QB), F32)],
        compiler_params=pltpu.CompilerParams(
            dimension_semantics=("parallel", "arbitrary"),
            vmem_limit_bytes=48 * 1024 * 1024),
        name="moba_attn",
    )(proj3, proj3, proj3)


def _dsa_kernel(q_ref, k_ref, v_ref, qx_ref, kxq_ref, kx_ref, o_ref,
                kx2_ref, key_ref, mb_ref, cidx_ref, qh_ref, acc_ref, *, top):
    qi = pl.program_id(1)
    seq = k_ref.shape[0]

    @pl.when(qi == 0)
    def _():
        kx = kx_ref[...].astype(F32)
        dup = jnp.where(_half_mask(kx.shape, 1, 0), kx, pltpu.roll(kx, HEAD_DIM, 1))
        kx2_ref[...] = dup.astype(BF16)

    row = lax.broadcasted_iota(I32, (KC, QB), 0)
    lane = lax.broadcasted_iota(I32, (KC, QB), 1)
    causal_diag = row <= lane
    key_minus_query = (row - lane).astype(F32)

    pick = (lax.broadcasted_iota(I32, (IDX_HEADS, LANES), 1)
            == lax.broadcasted_iota(I32, (IDX_HEADS, LANES), 0) + IDX_DIM).astype(BF16)
    w_t = lax.dot_general(pick, kxq_ref[...], NT, preferred_element_type=F32)

    def score_chunk(c, diag):
        start = pl.multiple_of(c * KC, KC)
        kc = kx2_ref[pl.ds(start, KC), :]
        sc = jnp.zeros((KC, QB), F32)
        for pp in range(IDX_HEADS // 2):
            qp = qx_ref[:, pp * LANES:(pp + 1) * LANES]
            for hh in range(2):
                h = 2 * pp + hh
                qh = jnp.where(_half_mask(qp.shape, 1, hh), qp, jnp.zeros_like(qp))
                lg = lax.dot_general(kc, qh, NT, preferred_element_type=F32)
                sc = sc + jnp.maximum(lg, 0.0) * w_t[h:h + 1, :]
        if diag:
            sc = jnp.where(causal_diag, sc, NEG_BIG)
        bits = pltpu.bitcast(sc, I32)
        key_ref[pl.ds(start, KC), :] = jnp.where(bits < 0, bits ^ jnp.int32(0x7FFFFFFF), bits)

    score_chunk(qi, True)

    def _score_body(c, carry):
        score_chunk(c, False)
        return carry

    lax.fori_loop(0, qi, _score_body, 0)

    def count(pred):
        def body(c, acc):
            start = pl.multiple_of(c * KC, KC)
            hit = jnp.where(pred(key_ref[pl.ds(start, KC), :], c), 1, 0)
            return acc + jnp.sum(hit.reshape(KC // 8, 8, QB), axis=0)
        acc8 = lax.fori_loop(0, qi + 1, body, jnp.zeros((8, QB), I32))
        return jnp.sum(acc8, axis=0, keepdims=True)

    def bit_step(i, tau):
        cand = tau + lax.shift_left(jnp.int32(1), 31 - i)
        cnt = count(lambda keys, c: keys >= cand)
        return jnp.where(cnt >= top, cand, tau)

    n_bits = jnp.where(qi > 0, 32, 0)
    tau = lax.fori_loop(0, n_bits, bit_step, jnp.full((1, QB), INT_MIN, I32))

    cnt_ge = count(lambda keys, c: keys >= tau)
    cnt_gt = count(lambda keys, c: keys > tau)
    need = top - cnt_gt
    cidx_ref[...] = jnp.full(cidx_ref.shape, seq, I32)

    @pl.when(jnp.max(cnt_ge) > top)
    def _():
        def idx_step(i, x):
            cand = x + lax.shift_left(jnp.int32(1), 11 - i)
            cnt = count(lambda keys, c: (keys == tau) & ((c * KC + row) < cand))
            return jnp.where(cnt < need, cand, x)
        x = lax.fori_loop(0, 12, idx_step, jnp.zeros((1, QB), I32))
        cidx_ref[...] = jnp.broadcast_to(x, cidx_ref.shape)

    cidx = cidx_ref[0:1, :]

    def mask_chunk(c, diag):
        start = pl.multiple_of(c * KC, KC)
        keys = key_ref[pl.ds(start, KC), :]
        keep = (keys > tau) | ((keys == tau) & ((c * KC + row) <= cidx))
        if diag:
            keep = keep & causal_diag
        mb_ref[pl.ds(start, KC), :] = jnp.where(keep, 0.0, NEG_BIG)

    mask_chunk(qi, True)

    def _mask_body(c, carry):
        mask_chunk(c, False)
        return carry

    lax.fori_loop(0, qi, _mask_body, 0)

    _masked_heads(q_ref, qh_ref, H_DSA)

    def chunk_bias(c, diag):
        start = pl.multiple_of(c * KC, KC)
        keep_bias = mb_ref[pl.ds(start, KC), :]
        dist = key_minus_query if diag else key_minus_query + ((c - qi) * KC).astype(F32)
        return lambda head: SLOPES_DSA[head] * dist + keep_bias

    _softmax_heads(qh_ref, k_ref, v_ref, acc_ref, o_ref, qi, H_DSA, chunk_bias)


def _dsa_attention(proj3):
    bsz, seq, _ = proj3.shape
    top = min(DSA_TOPK_MAX, seq // 4)
    assert top == QB, "the first query block must keep every admissible key"
    return pl.pallas_call(
        functools.partial(_dsa_kernel, top=top),
        grid=(bsz, seq // QB),
        in_specs=[
            pl.BlockSpec((None, QB, PAIR_W), lambda b, i: (b, i, 6)),
            pl.BlockSpec((None, seq, PAIR_W), lambda b, i: (b, 0, 7)),
            pl.BlockSpec((None, seq, PAIR_W), lambda b, i: (b, 0, 8)),
            pl.BlockSpec((None, QB, IDX_HEADS * IDX_DIM), lambda b, i: (b, i, COL_QX // (IDX_HEADS * IDX_DIM))),
            pl.BlockSpec((None, QB, LANES), lambda b, i: (b, i, COL_KX // LANES)),
            pl.BlockSpec((None, seq, LANES), lambda b, i: (b, 0, COL_KX // LANES)),
        ],
        out_specs=pl.BlockSpec((None, PAIR_W, QB), lambda b, i: (b, 0, i)),
        out_shape=jax.ShapeDtypeStruct((bsz, PAIR_W, seq), BF16),
        scratch_shapes=[pltpu.VMEM((seq, LANES), BF16),
                        pltpu.VMEM((seq, QB), I32),
                        pltpu.VMEM((seq, QB), F32),
                        pltpu.VMEM((8, QB), I32),
                        pltpu.VMEM((H_DSA, QB, LANES), BF16),
                        pltpu.VMEM((PAIR_W // LANES, LANES, QB), F32)],
        compiler_params=pltpu.CompilerParams(
            dimension_semantics=("parallel", "arbitrary"),
            vmem_limit_bytes=48 * 1024 * 1024),
        name="dsa_attn",
    )(proj3, proj3, proj3, proj3, proj3, proj3)


def _merge_kernel(osb_ref, omb_ref, ods_ref, gsb_ref, gmb_ref, gds_ref, x_ref, mod_ref, gpost_ref,
                  wsb_ref, wmb_ref, wds_ref, wo_ref, out_ref):
    def branch(o_ref, w_ref, g_ref):
        y = lax.dot_general(o_ref[...], w_ref[...], TN, preferred_element_type=F32)
        return g_ref[...].astype(F32) * y

    merged = branch(osb_ref, wsb_ref, gsb_ref) + branch(omb_ref, wmb_ref, gmb_ref) + branch(ods_ref, wds_ref, gds_ref)
    y = jnp.dot(merged.astype(BF16), wo_ref[...], preferred_element_type=F32)
    out_ref[...] = x_ref[...] + mod_ref[2:3, :] * (_rms(y) * gpost_ref[...])


def _merge(o_sb, o_mb, o_ds, proj, x2d, mod_l, g_post, w_sb, w_mb, w_ds, w_o, seq):
    t, d = x2d.shape
    tm = 512
    per_b = seq // tm
    gate_blk = COL_GATE // d
    o_spec = pl.BlockSpec((None, PAIR_W, tm), lambda i: (i // per_b, 0, i % per_b))
    w_spec = pl.BlockSpec((PAIR_W, d), lambda i: (0, 0))
    return pl.pallas_call(
        _merge_kernel,
        grid=(t // tm,),
        in_specs=[
            o_spec, o_spec, o_spec,
            pl.BlockSpec((tm, d), lambda i: (i, gate_blk)),
            pl.BlockSpec((tm, d), lambda i: (i, gate_blk + 1)),
            pl.BlockSpec((tm, d), lambda i: (i, gate_blk + 2)),
            pl.BlockSpec((tm, d), lambda i: (i, 0)),
            pl.BlockSpec((None, 6, d), lambda i: (i // per_b, 0, 0)),
            pl.BlockSpec((1, d), lambda i: (0, 0)),
            w_spec, w_spec, w_spec,
            pl.BlockSpec((d, d), lambda i: (0, 0)),
        ],
        out_specs=pl.BlockSpec((tm, d), lambda i: (i, 0)),
        out_shape=jax.ShapeDtypeStruct((t, d), F32),
        compiler_params=pltpu.CompilerParams(
            dimension_semantics=("parallel",),
            vmem_limit_bytes=48 * 1024 * 1024),
        name="merge_out",
    )(o_sb, o_mb, o_ds, proj, proj, proj, x2d, mod_l, g_post, w_sb, w_mb, w_ds, w_o)


def _ffn_kernel(x_ref, mod_ref, gpre_ref, gpost_ref, wg_ref, wu_ref, wd_ref, out_ref, h_ref, acc_ref):
    j = pl.program_id(1)

    @pl.when(j == 0)
    def _():
        xh = _rms(x_ref[...])
        h = (xh * gpre_ref[...]) * (1.0 + mod_ref[4:5, :]) + mod_ref[3:4, :]
        h_ref[...] = h.astype(BF16)
        acc_ref[...] = jnp.zeros_like(acc_ref)

    h = h_ref[...]
    gate = jnp.dot(h, wg_ref[...], preferred_element_type=F32)
    up = jnp.dot(h, wu_ref[...], preferred_element_type=F32)
    act = (gate * jax.nn.sigmoid(gate)) * up
    acc_ref[...] += jnp.dot(act.astype(BF16), wd_ref[...], preferred_element_type=F32)

    @pl.when(j == pl.num_programs(1) - 1)
    def _():
        y = acc_ref[...]
        out_ref[...] = x_ref[...] + mod_ref[5:6, :] * (_rms(y) * gpost_ref[...])


def _ffn(x2d, mod_l, g_pre, g_post, w_up, w_down, seq):
    t, d = x2d.shape
    tm = 512
    tf = D_FF // 2
    n_f = D_FF // tf
    per_b = seq // tm
    return pl.pallas_call(
        _ffn_kernel,
        grid=(t // tm, n_f),
        in_specs=[
            pl.BlockSpec((tm, d), lambda i, j: (i, 0)),
            pl.BlockSpec((None, 6, d), lambda i, j: (i // per_b, 0, 0)),
            pl.BlockSpec((1, d), lambda i, j: (0, 0)),
            pl.BlockSpec((1, d), lambda i, j: (0, 0)),
            pl.BlockSpec((d, tf), lambda i, j: (0, j)),
            pl.BlockSpec((d, tf), lambda i, j: (0, j + n_f)),
            pl.BlockSpec((tf, d), lambda i, j: (j, 0)),
        ],
        out_specs=pl.BlockSpec((tm, d), lambda i, j: (i, 0)),
        out_shape=jax.ShapeDtypeStruct((t, d), F32),
        scratch_shapes=[pltpu.VMEM((tm, d), BF16), pltpu.VMEM((tm, d), F32)],
        compiler_params=pltpu.CompilerParams(
            dimension_semantics=("parallel", "arbitrary"),
            vmem_limit_bytes=56 * 1024 * 1024),
        name="ffn",
    )(x2d, mod_l, g_pre, g_post, w_up, w_up, w_down)


def _pad_cols(w, n):
    return jnp.pad(w, ((0, 0), (0, n - w.shape[1])))


def _pad_rows(w, n):
    return jnp.pad(w, ((0, n - w.shape[0]), (0, 0)))


def _layout_w_in(w_in_l, b_gate_l):
    sizes = [W_SB] * 3 + [W_MOBA] * 3 + [W_DSA] * 3 + [IDX_HEADS * IDX_DIM, IDX_DIM, IDX_HEADS, N_BRANCH * D_MODEL]
    points = [int(v) for v in np.cumsum(sizes)[:-1]]
    (q_sb, k_sb, v_sb, q_mb, k_mb, v_mb, q_ds, k_ds, v_ds, q_ix, k_ix, w_ix, gates) = jnp.split(w_in_l, points, axis=1)
    scale = HEAD_DIM ** -0.5
    cols = [q_sb * scale, k_sb, v_sb,
            _pad_cols(q_mb * scale, PAIR_W), _pad_cols(k_mb, PAIR_W), _pad_cols(v_mb, PAIR_W),
            _pad_cols(q_ds * scale, PAIR_W), _pad_cols(k_ds, PAIR_W), _pad_cols(v_ds, PAIR_W),
            _pad_cols(jnp.concatenate([k_ix, w_ix], axis=1), LANES), q_ix, gates]
    w_all = jnp.concatenate(cols, axis=1).astype(BF16)
    bias = jnp.concatenate([jnp.zeros((COL_GATE,), F32), b_gate_l])[None, :]
    return w_all, bias


def kernel(x, c, w_ada, b_ada, g_pre_mix, g_post_mix, w_in, b_gate, w_proj_sb, w_proj_moba,
           w_proj_dsa, w_o, g_pre_ffn, g_post_ffn, w_up, w_down):
    bsz, seq, d = x.shape
    depth = w_ada.shape[0]
    assert d == D_MODEL and seq % QB == 0 and QB == MOBA_BLOCK and KC == QB
    mod = _ada(c, w_ada, b_ada).reshape(depth, bsz, 6, d)
    x2d = x.reshape(bsz * seq, d)
    for l in range(depth):
        w_all, bias = _layout_w_in(w_in[l], b_gate[l])
        assert w_all.shape[1] == N_PROJ
        proj = _inproj(x2d, mod[l], g_pre_mix[l][None, :], w_all, bias, seq)
        proj3 = proj.reshape(bsz, seq, N_PROJ)
        o_sb = _sb_attention(proj3)
        o_mb = _moba_attention(proj3)
        o_ds = _dsa_attention(proj3)
        x2d = _merge(o_sb, o_mb, o_ds, proj, x2d, mod[l], g_post_mix[l][None, :],
                     w_proj_sb[l].astype(BF16),
                     _pad_rows(w_proj_moba[l], PAIR_W).astype(BF16),
                     _pad_rows(w_proj_dsa[l], PAIR_W).astype(BF16),
                     w_o[l].astype(BF16), seq)
        x2d = _ffn(x2d, mod[l], g_pre_ffn[l][None, :], g_post_ffn[l][None, :],
                   w_up[l].astype(BF16), w_down[l].astype(BF16), seq)
    return x2d.reshape(bsz, seq, d)
```

```python
import functools

import numpy as np
import jax
import jax.numpy as jnp
from jax import lax
from jax.experimental import pallas as pl
from jax.experimental.pallas import tpu as pltpu

F32 = jnp.float32
BF16 = jnp.bfloat16
I32 = jnp.int32
I16 = jnp.int16

D_MODEL = 1024
HEAD_DIM = 64
H_SB, H_MOBA, H_DSA = 6, 5, 5
W_SB, W_MOBA, W_DSA = H_SB * HEAD_DIM, H_MOBA * HEAD_DIM, H_DSA * HEAD_DIM
MOBA_BLOCK = 256
MOBA_TOPK = 3
DSA_TOPK_MAX = 256
IDX_HEADS = 8
IDX_DIM = 64
D_FF = 2816
N_BRANCH = 3
RMS_EPS = 1e-6
NEG_BIG = -1e30
ALIBI_HEADS = H_MOBA + H_DSA

LANES = 128
PAIR_W = 3 * LANES
QB = 256
KC = 256
INT_MIN = -(2 ** 31)
I16_MIN = -(2 ** 15)
PACK16 = 16

COL_KX = 9 * PAIR_W
COL_QX = COL_KX + LANES
COL_GATE = COL_QX + IDX_HEADS * IDX_DIM
N_PROJ = COL_GATE + N_BRANCH * D_MODEL
PROJ_TN = 1024
FFN_TF = 256

NT = (((1,), (1,)), ((), ()))
TN = (((0,), (0,)), ((), ()))

LOG2E = 1.4426950408889634
_ALIBI = [float(2.0 ** (-8.0 * h / ALIBI_HEADS)) * LOG2E for h in range(1, ALIBI_HEADS + 1)]
SLOPES_MOBA = _ALIBI[0::2]
SLOPES_DSA = _ALIBI[1::2]


def _rms(x):
    return x * lax.rsqrt(jnp.mean(x * x, axis=-1, keepdims=True) + RMS_EPS)


def _half_mask(shape, lane_axis, hh):
    lane = lax.broadcasted_iota(I32, shape, lane_axis)
    return (lane < HEAD_DIM) if hh == 0 else (lane >= HEAD_DIM)


def _ada_kernel(c_ref, w_ref, b_ref, o_ref):
    c = c_ref[...]
    ca = c * jax.nn.sigmoid(c)
    o_ref[...] = jnp.dot(ca, w_ref[...], preferred_element_type=F32) + b_ref[...]


def _ada(c, w_ada, b_ada):
    depth, d, n = w_ada.shape
    bsz = c.shape[0]
    tn = 512
    return pl.pallas_call(
        _ada_kernel,
        grid=(depth, n // tn),
        in_specs=[
            pl.BlockSpec((bsz, d), lambda l, j: (0, 0)),
            pl.BlockSpec((None, d, tn), lambda l, j: (l, 0, j)),
            pl.BlockSpec((None, 1, tn), lambda l, j: (l, 0, j)),
        ],
        out_specs=pl.BlockSpec((None, bsz, tn), lambda l, j: (l, 0, j)),
        out_shape=jax.ShapeDtypeStruct((depth, bsz, n), F32),
        name="ada_mod",
    )(c, w_ada, b_ada.reshape(depth, 1, n))


def _inproj_kernel(x_ref, mod_ref, g_ref, w_ref, b_ref, o_ref, h_ref, *, n_plain):
    j = pl.program_id(1)

    @pl.when(j == 0)
    def _():
        xh = _rms(x_ref[...])
        h = (xh * g_ref[...]) * (1.0 + mod_ref[1:2, :]) + mod_ref[0:1, :]
        h_ref[...] = h.astype(BF16)

    acc = jnp.dot(h_ref[...], w_ref[...], preferred_element_type=F32) + b_ref[...]

    @pl.when(j < n_plain)
    def _():
        o_ref[...] = acc.astype(BF16)

    @pl.when(j >= n_plain)
    def _():
        o_ref[...] = jax.nn.sigmoid(acc).astype(BF16)


def _inproj(x2d, mod_l, g_pre, w_all, bias, seq):
    t, d = x2d.shape
    tm = 1024
    n = w_all.shape[1]
    per_b = seq // tm
    return pl.pallas_call(
        functools.partial(_inproj_kernel, n_plain=COL_GATE // PROJ_TN),
        grid=(t // tm, n // PROJ_TN),
        in_specs=[
            pl.BlockSpec((tm, d), lambda i, j: (i, 0)),
            pl.BlockSpec((None, 6, d), lambda i, j: (i // per_b, 0, 0)),
            pl.BlockSpec((1, d), lambda i, j: (0, 0)),
            pl.BlockSpec((d, PROJ_TN), lambda i, j: (0, j)),
            pl.BlockSpec((1, PROJ_TN), lambda i, j: (0, j)),
        ],
        out_specs=pl.BlockSpec((tm, PROJ_TN), lambda i, j: (i, j)),
        out_shape=jax.ShapeDtypeStruct((t, n), BF16),
        scratch_shapes=[pltpu.VMEM((tm, d), BF16)],
        compiler_params=pltpu.CompilerParams(
            dimension_semantics=("parallel", "arbitrary"),
            vmem_limit_bytes=48 * 1024 * 1024),
        name="in_proj",
    )(x2d, mod_l, g_pre, w_all, bias)


def _masked_heads(ref, dst_ref, n_heads):
    for head in range(n_heads):
        grp = ref[:, (head // 2) * LANES:(head // 2 + 1) * LANES]
        dst_ref[head] = jnp.where(_half_mask(grp.shape, 1, head % 2), grp, jnp.zeros_like(grp))


def _sb_kernel(q_ref, k_ref, v_ref, o_ref, qh_ref, acc_ref):
    qi = pl.program_id(1)
    row = lax.broadcasted_iota(I32, (KC, QB), 0)
    lane = lax.broadcasted_iota(I32, (KC, QB), 1)
    past_diag = row < lane
    upper = (lax.broadcasted_iota(I32, (KC, KC), 1) > lax.broadcasted_iota(I32, (KC, KC), 0)).astype(BF16)

    _masked_heads(q_ref, qh_ref, H_SB)
    acc_ref[...] = jnp.zeros_like(acc_ref)

    def chunk(c, carries, diag):
        start = pl.multiple_of(c * KC, KC)
        new_carries = []
        for p in range(H_SB // 2):
            cols = slice(p * LANES, (p + 1) * LANES)
            kc = k_ref[pl.ds(start, KC), cols]
            vc = v_ref[pl.ds(start, KC), cols]
            pv = None
            for hh in range(2):
                head = 2 * p + hh
                z = lax.dot_general(kc, qh_ref[head], NT, preferred_element_type=F32)
                neg_abs_z = pltpu.bitcast(pltpu.bitcast(z, I32) | jnp.int32(INT_MIN), F32)
                log_beta = jnp.minimum(z, 0.0) - jnp.log2(1.0 + jnp.exp2(neg_abs_z))
                log_1m = log_beta - z
                if diag:
                    log_1m = jnp.where(past_diag, log_1m, 0.0)
                between = jnp.dot(upper, log_1m.astype(BF16), preferred_element_type=F32)
                a = jnp.exp2(log_beta + between + carries[head])
                if diag:
                    a = jnp.where(past_diag, a, 0.0)
                vh = jnp.where(_half_mask(vc.shape, 1, hh), vc, jnp.zeros_like(vc))
                part = lax.dot_general(vh, a.astype(BF16), TN, preferred_element_type=F32)
                pv = part if pv is None else pv + part
                new_carries.append(carries[head] + jnp.sum(log_1m, axis=0, keepdims=True))
            acc_ref[p] += pv
        return tuple(new_carries)

    carries = chunk(qi, tuple(jnp.zeros((1, QB), F32) for _ in range(H_SB)), True)
    lax.fori_loop(0, qi, lambda i, cs: chunk(qi - 1 - i, cs, False), carries)
    for p in range(H_SB // 2):
        o_ref[p * LANES:(p + 1) * LANES, :] = acc_ref[p].astype(BF16)


def _sb_attention(proj3):
    bsz, seq, _ = proj3.shape
    return pl.pallas_call(
        _sb_kernel,
        grid=(bsz, seq // QB),
        in_specs=[
            pl.BlockSpec((None, QB, PAIR_W), lambda b, i: (b, i, 0)),
            pl.BlockSpec((None, seq, PAIR_W), lambda b, i: (b, 0, 1)),
            pl.BlockSpec((None, seq, PAIR_W), lambda b, i: (b, 0, 2)),
        ],
        out_specs=pl.BlockSpec((None, PAIR_W, QB), lambda b, i: (b, 0, i)),
        out_shape=jax.ShapeDtypeStruct((bsz, PAIR_W, seq), BF16),
        scratch_shapes=[pltpu.VMEM((H_SB, QB, LANES), BF16),
                        pltpu.VMEM((H_SB // 2, LANES, QB), F32)],
        compiler_params=pltpu.CompilerParams(
            dimension_semantics=("parallel", "arbitrary"),
            vmem_limit_bytes=48 * 1024 * 1024),
        name="sb_attn",
    )(proj3, proj3, proj3)


def _softmax_heads(qh_ref, k_ref, v_ref, acc_ref, o_ref, qi, n_heads, chunk_bias):
    out_row = lax.broadcasted_iota(I32, (LANES, QB), 0)
    n_pairs = (n_heads + 1) // 2
    acc_ref[...] = jnp.zeros_like(acc_ref)

    def by_head_rows(vals):
        return vals[0] if len(vals) == 1 else jnp.where(out_row < HEAD_DIM, vals[0], vals[1])

    def chunk(c, state, diag):
        ms, ls = state
        start = pl.multiple_of(c * KC, KC)
        bias_of = chunk_bias(c, diag)
        new_ms, new_ls = [], []
        for p in range(n_pairs):
            cols = slice(p * LANES, (p + 1) * LANES)
            kc = k_ref[pl.ds(start, KC), cols]
            vc = v_ref[pl.ds(start, KC), cols]
            heads = [h for h in (2 * p, 2 * p + 1) if h < n_heads]
            pv, alphas = None, []
            for head in heads:
                s = lax.dot_general(kc, qh_ref[head], NT, preferred_element_type=F32) + bias_of(head)
                m_new = jnp.maximum(ms[head], jnp.max(s, axis=0, keepdims=True))
                alpha = jnp.exp2(ms[head] - m_new)
                pr = jnp.exp2(s - m_new)
                new_ms.append(m_new)
                new_ls.append(alpha * ls[head] + jnp.sum(pr, axis=0, keepdims=True))
                alphas.append(alpha)
                vh = vc if len(heads) == 1 else jnp.where(_half_mask(vc.shape, 1, head % 2), vc, jnp.zeros_like(vc))
                part = lax.dot_general(vh, pr.astype(BF16), TN, preferred_element_type=F32)
                pv = part if pv is None else pv + part
            acc_ref[p] = acc_ref[p] * by_head_rows(alphas) + pv
        return tuple(new_ms), tuple(new_ls)

    init = (tuple(jnp.full((1, QB), NEG_BIG, F32) for _ in range(n_heads)),
            tuple(jnp.zeros((1, QB), F32) for _ in range(n_heads)))
    state = chunk(qi, init, True)
    _, ls = lax.fori_loop(0, qi, lambda c, st: chunk(c, st, False), state)
    for p in range(n_pairs):
        denom = by_head_rows([ls[h] for h in (2 * p, 2 * p + 1) if h < n_heads])
        o_ref[p * LANES:(p + 1) * LANES, :] = (acc_ref[p] / denom).astype(BF16)


def _moba_kernel(q_ref, k_ref, v_ref, o_ref, kmean_ref, sel_ref, qh_ref, acc_ref):
    qi = pl.program_id(1)
    n_blocks = kmean_ref.shape[0]

    @pl.when(qi == 0)
    def _():
        for n in range(n_blocks):
            kb = k_ref[n * MOBA_BLOCK:(n + 1) * MOBA_BLOCK, :].astype(F32)
            kmean_ref[n:n + 1, :] = jnp.mean(kb, axis=0, keepdims=True)

    row = lax.broadcasted_iota(I32, (KC, QB), 0)
    lane = lax.broadcasted_iota(I32, (KC, QB), 1)
    causal_diag = row <= lane
    key_minus_query = (row - lane).astype(F32)
    blk = lax.broadcasted_iota(I32, (n_blocks, QB), 0)

    _masked_heads(q_ref, qh_ref, H_MOBA)

    for head in range(H_MOBA):
        cols = slice((head // 2) * LANES, (head // 2 + 1) * LANES)
        gate = lax.dot_general(kmean_ref[:, cols], qh_ref[head].astype(F32), NT,
                               preferred_element_type=F32, precision=lax.Precision.HIGHEST)
        rank = jnp.zeros((n_blocks, QB), I32)
        for mb in range(n_blocks):
            gm = gate[mb:mb + 1, :]
            beats = (gm > gate) | ((gm == gate) & (mb < blk))
            rank = rank + jnp.where(beats, (mb < qi).astype(I32), 0)
        selected = (rank < MOBA_TOPK) & (blk < qi)
        sel_ref[head] = jnp.where(selected, 0.0, NEG_BIG)

    def chunk_bias(c, diag):
        if diag:
            return lambda head: jnp.where(causal_diag, SLOPES_MOBA[head] * key_minus_query, NEG_BIG)
        dist = key_minus_query + ((c - qi) * KC).astype(F32)
        return lambda head: SLOPES_MOBA[head] * dist + sel_ref[head, pl.ds(c, 1), :]

    _softmax_heads(qh_ref, k_ref, v_ref, acc_ref, o_ref, qi, H_MOBA, chunk_bias)


def _moba_attention(proj3):
    bsz, seq, _ = proj3.shape
    n_blocks = seq // MOBA_BLOCK
    return pl.pallas_call(
        _moba_kernel,
        grid=(bsz, seq // QB),
        in_specs=[
            pl.BlockSpec((None, QB, PAIR_W), lambda b, i: (b, i, 3)),
            pl.BlockSpec((None, seq, PAIR_W), lambda b, i: (b, 0, 4)),
            pl.BlockSpec((None, seq, PAIR_W), lambda b, i: (b, 0, 5)),
        ],
        out_specs=pl.BlockSpec((None, PAIR_W, QB), lambda b, i: (b, 0, i)),
        out_shape=jax.ShapeDtypeStruct((bsz, PAIR_W, seq), BF16),
        scratch_shapes=[pltpu.VMEM((n_blocks, PAIR_W), F32),
                        pltpu.VMEM((H_MOBA, n_blocks, QB), F32),
                        pltpu.VMEM((H_MOBA, QB, LANES), BF16),
                        pltpu.VMEM((PAIR_W // LANES, LANES, QB), F32)],
        compiler_params=pltpu.CompilerParams(
            dimension_semantics=("parallel", "arbitrary"),
            vmem_limit_bytes=48 * 1024 * 1024),
        name="moba_attn",
    )(proj3, proj3, proj3)


def _dsa_kernel(q_ref, k_ref, v_ref, qx_ref, kxq_ref, kx_ref, o_ref,
                kx2_ref, key_ref, hi_ref, lo_ref, mb_ref, cidx_ref, qh_ref, acc_ref, *, top):
    qi = pl.program_id(1)
    seq = k_ref.shape[0]

    @pl.when(qi == 0)
    def _():
        kx = kx_ref[...].astype(F32)
        dup = jnp.where(_half_mask(kx.shape, 1, 0), kx, pltpu.roll(kx, HEAD_DIM, 1))
        kx2_ref[...] = dup.astype(BF16)

    row = lax.broadcasted_iota(I32, (KC, QB), 0)
    lane = lax.broadcasted_iota(I32, (KC, QB), 1)
    causal_diag = row <= lane
    key_minus_query = (row - lane).astype(F32)

    pick = (lax.broadcasted_iota(I32, (IDX_HEADS, LANES), 1)
            == lax.broadcasted_iota(I32, (IDX_HEADS, LANES), 0) + IDX_DIM).astype(BF16)
    w_t = lax.dot_general(pick, kxq_ref[...], NT, preferred_element_type=F32)

    def score_chunk(c, diag):
        start = pl.multiple_of(c * KC, KC)
        kc = kx2_ref[pl.ds(start, KC), :]
        sc = jnp.zeros((KC, QB), F32)
        for pp in range(IDX_HEADS // 2):
            qp = qx_ref[:, pp * LANES:(pp + 1) * LANES]
            for hh in range(2):
                h = 2 * pp + hh
                qh = jnp.where(_half_mask(qp.shape, 1, hh), qp, jnp.zeros_like(qp))
                lg = lax.dot_general(kc, qh, NT, preferred_element_type=F32)
                sc = sc + jnp.maximum(lg, 0.0) * w_t[h:h + 1, :]
        if diag:
            sc = jnp.where(causal_diag, sc, NEG_BIG)
        bits = pltpu.bitcast(sc, I32)
        key = jnp.where(bits < 0, bits ^ jnp.int32(0x7FFFFFFF), bits)
        key_ref[pl.ds(start, KC), :] = key
        hi_ref[pl.ds(start, KC), :] = lax.shift_right_arithmetic(key, 16).astype(I16)
        lo_ref[pl.ds(start, KC), :] = ((key & 0xFFFF) + I16_MIN).astype(I16)

    score_chunk(qi, True)

    def _score_body(c, carry):
        score_chunk(c, False)
        return carry

    lax.fori_loop(0, qi, _score_body, 0)

    def count(pred):
        def body(c, acc):
            start = pl.multiple_of(c * KC, KC)
            hit = jnp.where(pred(key_ref[pl.ds(start, KC), :], c), 1, 0)
            return acc + jnp.sum(hit.reshape(KC // 8, 8, QB), axis=0)
        acc8 = lax.fori_loop(0, qi + 1, body, jnp.zeros((8, QB), I32))
        return jnp.sum(acc8, axis=0, keepdims=True)

    def packed_rows(x):
        return jnp.broadcast_to(x, (PACK16, QB)).astype(I16)[None]

    def count16(ref, pred):
        def body(c, acc):
            start = pl.multiple_of(c * KC, KC)
            hit = jnp.where(pred(ref[pl.ds(start, KC), :].reshape(KC // PACK16, PACK16, QB)),
                            jnp.int16(1), jnp.int16(0))
            for r in range(KC // PACK16):
                acc = acc + hit[r]
            return acc
        acc = lax.fori_loop(0, qi + 1, body, jnp.zeros((PACK16, QB), I16))
        return jnp.sum(acc.astype(I32), axis=0, keepdims=True)

    def kth_largest16(ref, kth, n_bits):
        def bit_step(i, tau):
            cand = tau + lax.shift_left(jnp.int32(1), 15 - i)
            cand16 = packed_rows(cand)
            cnt = count16(ref, lambda v: v >= cand16)
            return jnp.where(cnt >= kth, cand, tau)
        return lax.fori_loop(0, n_bits, bit_step, jnp.full((1, QB), I16_MIN, I32))

    n_bits = jnp.where(qi > 0, 16, 0)
    tau_hi = kth_largest16(hi_ref, top, n_bits)
    tau_hi16 = packed_rows(tau_hi)
    above = count16(hi_ref, lambda v: v > tau_hi16)

    def _park_body(c, carry):
        start = pl.multiple_of(c * KC, KC)
        shape3 = (KC // PACK16, PACK16, QB)
        hi = hi_ref[pl.ds(start, KC), :].reshape(shape3)
        lo = lo_ref[pl.ds(start, KC), :].reshape(shape3)
        lo_ref[pl.ds(start, KC), :] = jnp.where(hi == tau_hi16, lo, jnp.int16(I16_MIN)).reshape(KC, QB)
        return carry

    lax.fori_loop(0, qi + 1, _park_body, 0)
    tau_lo = kth_largest16(lo_ref, top - above, n_bits)
    tau = lax.shift_left(tau_hi, 16) | (tau_lo - I16_MIN)

    cnt_ge = count(lambda keys, c: keys >= tau)
    cnt_gt = count(lambda keys, c: keys > tau)
    need = top - cnt_gt
    cidx_ref[...] = jnp.full(cidx_ref.shape, seq, I32)

    @pl.when(jnp.max(cnt_ge) > top)
    def _():
        def idx_step(i, x):
            cand = x + lax.shift_left(jnp.int32(1), 11 - i)
            cnt = count(lambda keys, c: (keys == tau) & ((c * KC + row) < cand))
            return jnp.where(cnt < need, cand, x)
        x = lax.fori_loop(0, 12, idx_step, jnp.zeros((1, QB), I32))
        cidx_ref[...] = jnp.broadcast_to(x, cidx_ref.shape)

    cidx = cidx_ref[0:1, :]

    def mask_chunk(c, diag):
        start = pl.multiple_of(c * KC, KC)
        keys = key_ref[pl.ds(start, KC), :]
        keep = (keys > tau) | ((keys == tau) & ((c * KC + row) <= cidx))
        if diag:
            keep = keep & causal_diag
        mb_ref[pl.ds(start, KC), :] = jnp.where(keep, 0.0, NEG_BIG)

    mask_chunk(qi, True)

    def _mask_body(c, carry):
        mask_chunk(c, False)
        return carry

    lax.fori_loop(0, qi, _mask_body, 0)

    _masked_heads(q_ref, qh_ref, H_DSA)

    def chunk_bias(c, diag):
        start = pl.multiple_of(c * KC, KC)
        keep_bias = mb_ref[pl.ds(start, KC), :]
        dist = key_minus_query if diag else key_minus_query + ((c - qi) * KC).astype(F32)
        return lambda head: SLOPES_DSA[head] * dist + keep_bias

    _softmax_heads(qh_ref, k_ref, v_ref, acc_ref, o_ref, qi, H_DSA, chunk_bias)


def _dsa_attention(proj3):
    bsz, seq, _ = proj3.shape
    top = min(DSA_TOPK_MAX, seq // 4)
    assert top == QB, "the first query block must keep every admissible key"
    return pl.pallas_call(
        functools.partial(_dsa_kernel, top=top),
        grid=(bsz, seq // QB),
        in_specs=[
            pl.BlockSpec((None, QB, PAIR_W), lambda b, i: (b, i, 6)),
            pl.BlockSpec((None, seq, PAIR_W), lambda b, i: (b, 0, 7)),
            pl.BlockSpec((None, seq, PAIR_W), lambda b, i: (b, 0, 8)),
            pl.BlockSpec((None, QB, IDX_HEADS * IDX_DIM), lambda b, i: (b, i, COL_QX // (IDX_HEADS * IDX_DIM))),
            pl.BlockSpec((None, QB, LANES), lambda b, i: (b, i, COL_KX // LANES)),
            pl.BlockSpec((None, seq, LANES), lambda b, i: (b, 0, COL_KX // LANES)),
        ],
        out_specs=pl.BlockSpec((None, PAIR_W, QB), lambda b, i: (b, 0, i)),
        out_shape=jax.ShapeDtypeStruct((bsz, PAIR_W, seq), BF16),
        scratch_shapes=[pltpu.VMEM((seq, LANES), BF16),
                        pltpu.VMEM((seq, QB), I32),
                        pltpu.VMEM((seq, QB), I16),
                        pltpu.VMEM((seq, QB), I16),
                        pltpu.VMEM((seq, QB), F32),
                        pltpu.VMEM((8, QB), I32),
                        pltpu.VMEM((H_DSA, QB, LANES), BF16),
                        pltpu.VMEM((PAIR_W // LANES, LANES, QB), F32)],
        compiler_params=pltpu.CompilerParams(
            dimension_semantics=("parallel", "arbitrary"),
            vmem_limit_bytes=48 * 1024 * 1024),
        name="dsa_attn",
    )(proj3, proj3, proj3, proj3, proj3, proj3)


def _merge_kernel(osb_ref, omb_ref, ods_ref, gsb_ref, gmb_ref, gds_ref, x_ref, mod_ref, gpost_ref,
                  wsb_ref, wmb_ref, wds_ref, wo_ref, out_ref):
    def branch(o_ref, w_ref, g_ref):
        y = lax.dot_general(o_ref[...], w_ref[...], TN, preferred_element_type=F32)
        return g_ref[...].astype(F32) * y

    merged = branch(osb_ref, wsb_ref, gsb_ref) + branch(omb_ref, wmb_ref, gmb_ref) + branch(ods_ref, wds_ref, gds_ref)
    y = jnp.dot(merged.astype(BF16), wo_ref[...], preferred_element_type=F32)
    out_ref[...] = x_ref[...] + mod_ref[2:3, :] * (_rms(y) * gpost_ref[...])


def _merge(o_sb, o_mb, o_ds, proj, x2d, mod_l, g_post, w_sb, w_mb, w_ds, w_o, seq):
    t, d = x2d.shape
    tm = 512
    per_b = seq // tm
    gate_blk = COL_GATE // d
    o_spec = pl.BlockSpec((None, PAIR_W, tm), lambda i: (i // per_b, 0, i % per_b))
    w_spec = pl.BlockSpec((PAIR_W, d), lambda i: (0, 0))
    return pl.pallas_call(
        _merge_kernel,
        grid=(t // tm,),
        in_specs=[
            o_spec, o_spec, o_spec,
            pl.BlockSpec((tm, d), lambda i: (i, gate_blk)),
            pl.BlockSpec((tm, d), lambda i: (i, gate_blk + 1)),
            pl.BlockSpec((tm, d), lambda i: (i, gate_blk + 2)),
            pl.BlockSpec((tm, d), lambda i: (i, 0)),
            pl.BlockSpec((None, 6, d), lambda i: (i // per_b, 0, 0)),
            pl.BlockSpec((1, d), lambda i: (0, 0)),
            w_spec, w_spec, w_spec,
            pl.BlockSpec((d, d), lambda i: (0, 0)),
        ],
        out_specs=pl.BlockSpec((tm, d), lambda i: (i, 0)),
        out_shape=jax.ShapeDtypeStruct((t, d), F32),
        compiler_params=pltpu.CompilerParams(
            dimension_semantics=("parallel",),
            vmem_limit_bytes=48 * 1024 * 1024),
        name="merge_out",
    )(o_sb, o_mb, o_ds, proj, proj, proj, x2d, mod_l, g_post, w_sb, w_mb, w_ds, w_o)


def _ffn_kernel(x_ref, mod_ref, gpre_ref, gpost_ref, wg_ref, wu_ref, wd_ref, out_ref, h_ref, acc_ref):
    xh = _rms(x_ref[...])
    h_ref[...] = ((xh * gpre_ref[...]) * (1.0 + mod_ref[4:5, :]) + mod_ref[3:4, :]).astype(BF16)
    for f in range(0, D_FF, FFN_TF):
        h = h_ref[...]
        gate = jnp.dot(h, wg_ref[:, f:f + FFN_TF], preferred_element_type=F32)
        up = jnp.dot(h, wu_ref[:, f:f + FFN_TF], preferred_element_type=F32)
        act = ((gate * jax.nn.sigmoid(gate)) * up).astype(BF16)
        part = jnp.dot(act, wd_ref[f:f + FFN_TF, :], preferred_element_type=F32)
        if f == 0:
            acc_ref[...] = part
        else:
            acc_ref[...] += part
    y = acc_ref[...]
    out_ref[...] = x_ref[...] + mod_ref[5:6, :] * (_rms(y) * gpost_ref[...])


def _ffn(x2d, mod_l, g_pre, g_post, w_up, w_down, seq):
    t, d = x2d.shape
    tm = 512
    per_b = seq // tm
    return pl.pallas_call(
        _ffn_kernel,
        grid=(t // tm,),
        in_specs=[
            pl.BlockSpec((tm, d), lambda i: (i, 0)),
            pl.BlockSpec((None, 6, d), lambda i: (i // per_b, 0, 0)),
            pl.BlockSpec((1, d), lambda i: (0, 0)),
            pl.BlockSpec((1, d), lambda i: (0, 0)),
            pl.BlockSpec((d, D_FF), lambda i: (0, 0)),
            pl.BlockSpec((d, D_FF), lambda i: (0, 1)),
            pl.BlockSpec((D_FF, d), lambda i: (0, 0)),
        ],
        out_specs=pl.BlockSpec((tm, d), lambda i: (i, 0)),
        out_shape=jax.ShapeDtypeStruct((t, d), F32),
        scratch_shapes=[pltpu.VMEM((tm, d), BF16), pltpu.VMEM((tm, d), F32)],
        compiler_params=pltpu.CompilerParams(
            dimension_semantics=("parallel",),
            vmem_limit_bytes=56 * 1024 * 1024),
        name="ffn",
    )(x2d, mod_l, g_pre, g_post, w_up, w_up, w_down)


def _pad_cols(w, n):
    return jnp.pad(w, ((0, 0), (0, n - w.shape[1])))


def _pad_rows(w, n):
    return jnp.pad(w, ((0, n - w.shape[0]), (0, 0)))


def _layout_w_in(w_in_l, b_gate_l):
    sizes = [W_SB] * 3 + [W_MOBA] * 3 + [W_DSA] * 3 + [IDX_HEADS * IDX_DIM, IDX_DIM, IDX_HEADS, N_BRANCH * D_MODEL]
    points = [int(v) for v in np.cumsum(sizes)[:-1]]
    (q_sb, k_sb, v_sb, q_mb, k_mb, v_mb, q_ds, k_ds, v_ds, q_ix, k_ix, w_ix, gates) = jnp.split(w_in_l, points, axis=1)
    scale = HEAD_DIM ** -0.5 * LOG2E
    cols = [q_sb * scale, k_sb, v_sb,
            _pad_cols(q_mb * scale, PAIR_W), _pad_cols(k_mb, PAIR_W), _pad_cols(v_mb, PAIR_W),
            _pad_cols(q_ds * scale, PAIR_W), _pad_cols(k_ds, PAIR_W), _pad_cols(v_ds, PAIR_W),
            _pad_cols(jnp.concatenate([k_ix, w_ix], axis=1), LANES), q_ix, gates]
    w_all = jnp.concatenate(cols, axis=1).astype(BF16)
    bias = jnp.concatenate([jnp.zeros((COL_GATE,), F32), b_gate_l])[None, :]
    return w_all, bias


def kernel(x, c, w_ada, b_ada, g_pre_mix, g_post_mix, w_in, b_gate, w_proj_sb, w_proj_moba,
           w_proj_dsa, w_o, g_pre_ffn, g_post_ffn, w_up, w_down):
    bsz, seq, d = x.shape
    depth = w_ada.shape[0]
    assert d == D_MODEL and seq % QB == 0 and QB == MOBA_BLOCK and KC == QB
    mod = _ada(c, w_ada, b_ada).reshape(depth, bsz, 6, d)
    x2d = x.reshape(bsz * seq, d)
    for l in range(depth):
        w_all, bias = _layout_w_in(w_in[l], b_gate[l])
        assert w_all.shape[1] == N_PROJ
        proj = _inproj(x2d, mod[l], g_pre_mix[l][None, :], w_all, bias, seq)
        proj3 = proj.reshape(bsz, seq, N_PROJ)
        o_sb = _sb_attention(proj3)
        o_mb = _moba_attention(proj3)
        o_ds = _dsa_attention(proj3)
        x2d = _merge(o_sb, o_mb, o_ds, proj, x2d, mod[l], g_post_mix[l][None, :],
                     w_proj_sb[l].astype(BF16),
                     _pad_rows(w_proj_moba[l], PAIR_W).astype(BF16),
                     _pad_rows(w_proj_dsa[l], PAIR_W).astype(BF16),
                     w_o[l].astype(BF16), seq)
        x2d = _ffn(x2d, mod[l], g_pre_ffn[l][None, :], g_post_ffn[l][None, :],
                   w_up[l].astype(BF16), w_down[l].astype(BF16), seq)
    return x2d.reshape(bsz, seq, d)
```

```python
import functools

import numpy as np
import jax
import jax.numpy as jnp
from jax import lax
from jax.experimental import pallas as pl
from jax.experimental.pallas import tpu as pltpu

F32 = jnp.float32
BF16 = jnp.bfloat16
I32 = jnp.int32
I16 = jnp.int16

D_MODEL = 1024
HEAD_DIM = 64
H_SB, H_MOBA, H_DSA = 6, 5, 5
W_SB, W_MOBA, W_DSA = H_SB * HEAD_DIM, H_MOBA * HEAD_DIM, H_DSA * HEAD_DIM
MOBA_BLOCK = 256
MOBA_TOPK = 3
DSA_TOPK_MAX = 256
IDX_HEADS = 8
IDX_DIM = 64
D_FF = 2816
N_BRANCH = 3
RMS_EPS = 1e-6
NEG_BIG = -1e30
ALIBI_HEADS = H_MOBA + H_DSA

LANES = 128
PAIR_W = 3 * LANES
QB = 256
KC = 256
INT_MIN = -(2 ** 31)
I16_MIN = -(2 ** 15)
PACK16 = 16

COL_KX = 9 * PAIR_W
COL_QX = COL_KX + LANES
COL_GATE = COL_QX + IDX_HEADS * IDX_DIM
N_PROJ = COL_GATE + N_BRANCH * D_MODEL
PROJ_TN = 1024
FFN_TF = 256

NT = (((1,), (1,)), ((), ()))
TN = (((0,), (0,)), ((), ()))

LOG2E = 1.4426950408889634
_ALIBI = [float(2.0 ** (-8.0 * h / ALIBI_HEADS)) * LOG2E for h in range(1, ALIBI_HEADS + 1)]
SLOPES_MOBA = _ALIBI[0::2]
SLOPES_DSA = _ALIBI[1::2]


def _rms(x):
    return x * lax.rsqrt(jnp.mean(x * x, axis=-1, keepdims=True) + RMS_EPS)


def _fori_by_two(n, body, init):
    def two(i, state):
        return body(2 * i + 1, body(2 * i, state))
    state = lax.fori_loop(0, n // 2, two, init)
    return lax.fori_loop(2 * (n // 2), n, body, state)


def _half_mask(shape, lane_axis, hh):
    lane = lax.broadcasted_iota(I32, shape, lane_axis)
    return (lane < HEAD_DIM) if hh == 0 else (lane >= HEAD_DIM)


def _ada_kernel(c_ref, w_ref, b_ref, o_ref):
    c = c_ref[...]
    ca = c * jax.nn.sigmoid(c)
    o_ref[...] = jnp.dot(ca, w_ref[...], preferred_element_type=F32) + b_ref[...]


def _ada(c, w_ada, b_ada):
    depth, d, n = w_ada.shape
    bsz = c.shape[0]
    tn = 512
    return pl.pallas_call(
        _ada_kernel,
        grid=(depth, n // tn),
        in_specs=[
            pl.BlockSpec((bsz, d), lambda l, j: (0, 0)),
            pl.BlockSpec((None, d, tn), lambda l, j: (l, 0, j)),
            pl.BlockSpec((None, 1, tn), lambda l, j: (l, 0, j)),
        ],
        out_specs=pl.BlockSpec((None, bsz, tn), lambda l, j: (l, 0, j)),
        out_shape=jax.ShapeDtypeStruct((depth, bsz, n), F32),
        name="ada_mod",
    )(c, w_ada, b_ada.reshape(depth, 1, n))


def _inproj_kernel(x_ref, mod_ref, g_ref, w_ref, b_ref, o_ref):
    xh = _rms(x_ref[...])
    h = ((xh * g_ref[...]) * (1.0 + mod_ref[1:2, :]) + mod_ref[0:1, :]).astype(BF16)
    for n0 in range(0, N_PROJ, PROJ_TN):
        cols = slice(n0, n0 + PROJ_TN)
        acc = jnp.dot(h, w_ref[:, cols], preferred_element_type=F32) + b_ref[:, cols]
        if n0 >= COL_GATE:
            acc = jax.nn.sigmoid(acc)
        o_ref[:, cols] = acc.astype(BF16)


def _inproj(x2d, mod_l, g_pre, w_all, bias, seq):
    t, d = x2d.shape
    tm = 512
    n = w_all.shape[1]
    per_b = seq // tm
    return pl.pallas_call(
        _inproj_kernel,
        grid=(t // tm,),
        in_specs=[
            pl.BlockSpec((tm, d), lambda i: (i, 0)),
            pl.BlockSpec((None, 6, d), lambda i: (i // per_b, 0, 0)),
            pl.BlockSpec((1, d), lambda i: (0, 0)),
            pl.BlockSpec((d, n), lambda i: (0, 0)),
            pl.BlockSpec((1, n), lambda i: (0, 0)),
        ],
        out_specs=pl.BlockSpec((tm, n), lambda i: (i, 0)),
        out_shape=jax.ShapeDtypeStruct((t, n), BF16),
        compiler_params=pltpu.CompilerParams(
            dimension_semantics=("parallel",),
            vmem_limit_bytes=56 * 1024 * 1024),
        name="in_proj",
    )(x2d, mod_l, g_pre, w_all, bias)


def _masked_heads(ref, dst_ref, n_heads):
    for head in range(n_heads):
        grp = ref[:, (head // 2) * LANES:(head // 2 + 1) * LANES]
        dst_ref[head] = jnp.where(_half_mask(grp.shape, 1, head % 2), grp, jnp.zeros_like(grp))


def _sb_kernel(q_ref, k_ref, v_ref, o_ref, qh_ref, acc_ref):
    qi = pl.program_id(1)
    row = lax.broadcasted_iota(I32, (KC, QB), 0)
    lane = lax.broadcasted_iota(I32, (KC, QB), 1)
    past_diag = row < lane
    upper = (lax.broadcasted_iota(I32, (KC, KC), 1) > lax.broadcasted_iota(I32, (KC, KC), 0)).astype(BF16)

    _masked_heads(q_ref, qh_ref, H_SB)
    acc_ref[...] = jnp.zeros_like(acc_ref)

    def chunk(c, carries, diag):
        start = pl.multiple_of(c * KC, KC)
        new_carries = []
        for p in range(H_SB // 2):
            cols = slice(p * LANES, (p + 1) * LANES)
            kc = k_ref[pl.ds(start, KC), cols]
            vc = v_ref[pl.ds(start, KC), cols]
            pv = None
            for hh in range(2):
                head = 2 * p + hh
                z = lax.dot_general(kc, qh_ref[head], NT, preferred_element_type=F32)
                neg_abs_z = pltpu.bitcast(pltpu.bitcast(z, I32) | jnp.int32(INT_MIN), F32)
                log_beta = jnp.minimum(z, 0.0) - jnp.log2(1.0 + jnp.exp2(neg_abs_z))
                log_1m = log_beta - z
                if diag:
                    log_1m = jnp.where(past_diag, log_1m, 0.0)
                between = jnp.dot(upper, log_1m.astype(BF16), preferred_element_type=F32)
                a = jnp.exp2(log_beta + between + carries[head])
                if diag:
                    a = jnp.where(past_diag, a, 0.0)
                vh = jnp.where(_half_mask(vc.shape, 1, hh), vc, jnp.zeros_like(vc))
                part = lax.dot_general(vh, a.astype(BF16), TN, preferred_element_type=F32)
                pv = part if pv is None else pv + part
                new_carries.append(carries[head] + jnp.sum(log_1m, axis=0, keepdims=True))
            acc_ref[p] += pv
        return tuple(new_carries)

    carries = chunk(qi, tuple(jnp.zeros((1, QB), F32) for _ in range(H_SB)), True)
    _fori_by_two(qi, lambda i, cs: chunk(qi - 1 - i, cs, False), carries)
    for p in range(H_SB // 2):
        o_ref[p * LANES:(p + 1) * LANES, :] = acc_ref[p].astype(BF16)


def _sb_attention(proj3):
    bsz, seq, _ = proj3.shape
    return pl.pallas_call(
        _sb_kernel,
        grid=(bsz, seq // QB),
        in_specs=[
            pl.BlockSpec((None, QB, PAIR_W), lambda b, i: (b, i, 0)),
            pl.BlockSpec((None, seq, PAIR_W), lambda b, i: (b, 0, 1)),
            pl.BlockSpec((None, seq, PAIR_W), lambda b, i: (b, 0, 2)),
        ],
        out_specs=pl.BlockSpec((None, PAIR_W, QB), lambda b, i: (b, 0, i)),
        out_shape=jax.ShapeDtypeStruct((bsz, PAIR_W, seq), BF16),
        scratch_shapes=[pltpu.VMEM((H_SB, QB, LANES), BF16),
                        pltpu.VMEM((H_SB // 2, LANES, QB), F32)],
        compiler_params=pltpu.CompilerParams(
            dimension_semantics=("parallel", "arbitrary"),
            vmem_limit_bytes=48 * 1024 * 1024),
        name="sb_attn",
    )(proj3, proj3, proj3)


def _softmax_heads(qh_ref, k_ref, v_ref, acc_ref, o_ref, qi, n_heads, chunk_bias):
    out_row = lax.broadcasted_iota(I32, (LANES, QB), 0)
    n_pairs = (n_heads + 1) // 2
    acc_ref[...] = jnp.zeros_like(acc_ref)

    def by_head_rows(vals):
        return vals[0] if len(vals) == 1 else jnp.where(out_row < HEAD_DIM, vals[0], vals[1])

    def chunk(c, state, diag):
        ms, ls = state
        start = pl.multiple_of(c * KC, KC)
        bias_of = chunk_bias(c, diag)
        new_ms, new_ls = [], []
        for p in range(n_pairs):
            cols = slice(p * LANES, (p + 1) * LANES)
            kc = k_ref[pl.ds(start, KC), cols]
            vc = v_ref[pl.ds(start, KC), cols]
            heads = [h for h in (2 * p, 2 * p + 1) if h < n_heads]
            pv, alphas = None, []
            for head in heads:
                s = lax.dot_general(kc, qh_ref[head], NT, preferred_element_type=F32) + bias_of(head)
                m_new = jnp.maximum(ms[head], jnp.max(s, axis=0, keepdims=True))
                alpha = jnp.exp2(ms[head] - m_new)
                pr = jnp.exp2(s - m_new)
                new_ms.append(m_new)
                new_ls.append(alpha * ls[head] + jnp.sum(pr, axis=0, keepdims=True))
                alphas.append(alpha)
                vh = vc if len(heads) == 1 else jnp.where(_half_mask(vc.shape, 1, head % 2), vc, jnp.zeros_like(vc))
                part = lax.dot_general(vh, pr.astype(BF16), TN, preferred_element_type=F32)
                pv = part if pv is None else pv + part
            acc_ref[p] = acc_ref[p] * by_head_rows(alphas) + pv
        return tuple(new_ms), tuple(new_ls)

    init = (tuple(jnp.full((1, QB), NEG_BIG, F32) for _ in range(n_heads)),
            tuple(jnp.zeros((1, QB), F32) for _ in range(n_heads)))
    state = chunk(qi, init, True)
    _, ls = _fori_by_two(qi, lambda c, st: chunk(c, st, False), state)
    for p in range(n_pairs):
        denom = by_head_rows([ls[h] for h in (2 * p, 2 * p + 1) if h < n_heads])
        o_ref[p * LANES:(p + 1) * LANES, :] = (acc_ref[p] / denom).astype(BF16)


def _moba_kernel(q_ref, k_ref, v_ref, o_ref, kmean_ref, sel_ref, qh_ref, acc_ref):
    qi = pl.program_id(1)
    n_blocks = kmean_ref.shape[0]

    @pl.when(qi == 0)
    def _():
        for n in range(n_blocks):
            kb = k_ref[n * MOBA_BLOCK:(n + 1) * MOBA_BLOCK, :].astype(F32)
            kmean_ref[n:n + 1, :] = jnp.mean(kb, axis=0, keepdims=True)

    row = lax.broadcasted_iota(I32, (KC, QB), 0)
    lane = lax.broadcasted_iota(I32, (KC, QB), 1)
    causal_diag = row <= lane
    key_minus_query = (row - lane).astype(F32)
    blk = lax.broadcasted_iota(I32, (n_blocks, QB), 0)

    _masked_heads(q_ref, qh_ref, H_MOBA)

    for head in range(H_MOBA):
        cols = slice((head // 2) * LANES, (head // 2 + 1) * LANES)
        gate = lax.dot_general(kmean_ref[:, cols], qh_ref[head].astype(F32), NT,
                               preferred_element_type=F32, precision=lax.Precision.HIGHEST)
        rank = jnp.zeros((n_blocks, QB), I32)
        for mb in range(n_blocks):
            gm = gate[mb:mb + 1, :]
            beats = (gm > gate) | ((gm == gate) & (mb < blk))
            rank = rank + jnp.where(beats, (mb < qi).astype(I32), 0)
        selected = (rank < MOBA_TOPK) & (blk < qi)
        sel_ref[head] = jnp.where(selected, 0.0, NEG_BIG)

    def chunk_bias(c, diag):
        if diag:
            return lambda head: jnp.where(causal_diag, SLOPES_MOBA[head] * key_minus_query, NEG_BIG)
        dist = key_minus_query + ((c - qi) * KC).astype(F32)
        return lambda head: SLOPES_MOBA[head] * dist + sel_ref[head, pl.ds(c, 1), :]

    _softmax_heads(qh_ref, k_ref, v_ref, acc_ref, o_ref, qi, H_MOBA, chunk_bias)


def _moba_attention(proj3):
    bsz, seq, _ = proj3.shape
    n_blocks = seq // MOBA_BLOCK
    return pl.pallas_call(
        _moba_kernel,
        grid=(bsz, seq // QB),
        in_specs=[
            pl.BlockSpec((None, QB, PAIR_W), lambda b, i: (b, i, 3)),
            pl.BlockSpec((None, seq, PAIR_W), lambda b, i: (b, 0, 4)),
            pl.BlockSpec((None, seq, PAIR_W), lambda b, i: (b, 0, 5)),
        ],
        out_specs=pl.BlockSpec((None, PAIR_W, QB), lambda b, i: (b, 0, i)),
        out_shape=jax.ShapeDtypeStruct((bsz, PAIR_W, seq), BF16),
        scratch_shapes=[pltpu.VMEM((n_blocks, PAIR_W), F32),
                        pltpu.VMEM((H_MOBA, n_blocks, QB), F32),
                        pltpu.VMEM((H_MOBA, QB, LANES), BF16),
                        pltpu.VMEM((PAIR_W // LANES, LANES, QB), F32)],
        compiler_params=pltpu.CompilerParams(
            dimension_semantics=("parallel", "arbitrary"),
            vmem_limit_bytes=48 * 1024 * 1024),
        name="moba_attn",
    )(proj3, proj3, proj3)


def _dsa_kernel(q_ref, k_ref, v_ref, qx_ref, kxq_ref, kx_ref, o_ref,
                kx2_ref, key_ref, hi_ref, lo_ref, mb_ref, tau_ref, cidx_ref, qh_ref, acc_ref, *, top):
    qi = pl.program_id(1)
    seq = k_ref.shape[0]

    @pl.when(qi == 0)
    def _():
        kx = kx_ref[...].astype(F32)
        dup = jnp.where(_half_mask(kx.shape, 1, 0), kx, pltpu.roll(kx, HEAD_DIM, 1))
        kx2_ref[...] = dup.astype(BF16)

    row = lax.broadcasted_iota(I32, (KC, QB), 0)
    lane = lax.broadcasted_iota(I32, (KC, QB), 1)
    causal_diag = row <= lane
    key_minus_query = (row - lane).astype(F32)

    pick = (lax.broadcasted_iota(I32, (IDX_HEADS, LANES), 1)
            == lax.broadcasted_iota(I32, (IDX_HEADS, LANES), 0) + IDX_DIM).astype(BF16)
    w_t = lax.dot_general(pick, kxq_ref[...], NT, preferred_element_type=F32)

    def score_chunk(c, diag):
        start = pl.multiple_of(c * KC, KC)
        kc = kx2_ref[pl.ds(start, KC), :]
        sc = jnp.zeros((KC, QB), F32)
        for pp in range(IDX_HEADS // 2):
            qp = qx_ref[:, pp * LANES:(pp + 1) * LANES]
            for hh in range(2):
                h = 2 * pp + hh
                qh = jnp.where(_half_mask(qp.shape, 1, hh), qp, jnp.zeros_like(qp))
                lg = lax.dot_general(kc, qh, NT, preferred_element_type=F32)
                sc = sc + jnp.maximum(lg, 0.0) * w_t[h:h + 1, :]
        if diag:
            sc = jnp.where(causal_diag, sc, NEG_BIG)
        bits = pltpu.bitcast(sc, I32)
        key = jnp.where(bits < 0, bits ^ jnp.int32(0x7FFFFFFF), bits)
        key_ref[pl.ds(start, KC), :] = key
        hi_ref[pl.ds(start, KC), :] = lax.shift_right_arithmetic(key, 16).astype(I16)
        lo_ref[pl.ds(start, KC), :] = ((key & 0xFFFF) + I16_MIN).astype(I16)

    score_chunk(qi, True)

    def _score_body(c, carry):
        score_chunk(c, False)
        return carry

    lax.fori_loop(0, qi, _score_body, 0)

    shape16 = (KC // PACK16, PACK16, QB)

    def count(n, pred):
        acc8 = jnp.zeros((8, QB), I32)
        for c in range(n):
            hit = jnp.where(pred(key_ref[c * KC:(c + 1) * KC, :], c), 1, 0)
            acc8 = acc8 + jnp.sum(hit.reshape(KC // 8, 8, QB), axis=0)
        return jnp.sum(acc8, axis=0, keepdims=True)

    def packed_rows(x):
        return jnp.broadcast_to(x, (PACK16, QB)).astype(I16)[None]

    def count16(ref, n, pred):
        acc = jnp.zeros((PACK16, QB), I16)
        for c in range(n):
            hit = jnp.where(pred(ref[c * KC:(c + 1) * KC, :].reshape(shape16)), jnp.int16(1), jnp.int16(0))
            for r in range(KC // PACK16):
                acc = acc + hit[r]
        return jnp.sum(acc.astype(I32), axis=0, keepdims=True)

    def kth_largest16(ref, n, kth):
        def bit_step(i, tau):
            cand = tau + lax.shift_left(jnp.int32(1), 15 - i)
            cand16 = packed_rows(cand)
            cnt = count16(ref, n, lambda v: v >= cand16)
            return jnp.where(cnt >= kth, cand, tau)
        return lax.fori_loop(0, 16, bit_step, jnp.full((1, QB), I16_MIN, I32))

    def find_threshold(n):
        tau_hi = kth_largest16(hi_ref, n, top)
        tau_hi16 = packed_rows(tau_hi)
        above = count16(hi_ref, n, lambda v: v > tau_hi16)
        for c in range(n):
            rows = slice(c * KC, (c + 1) * KC)
            parked = jnp.where(hi_ref[rows, :].reshape(shape16) == tau_hi16,
                               lo_ref[rows, :].reshape(shape16), jnp.int16(I16_MIN))
            lo_ref[rows, :] = parked.reshape(KC, QB)
        tau_lo = kth_largest16(lo_ref, n, top - above)
        tau = lax.shift_left(tau_hi, 16) | (tau_lo - I16_MIN)
        tau_ref[...] = jnp.broadcast_to(tau, tau_ref.shape)

        cnt_ge = count(n, lambda keys, c: keys >= tau)
        cnt_gt = count(n, lambda keys, c: keys > tau)
        need = top - cnt_gt

        @pl.when(jnp.max(cnt_ge) > top)
        def _():
            def idx_step(i, x):
                cand = x + lax.shift_left(jnp.int32(1), 11 - i)
                cnt = count(n, lambda keys, c: (keys == tau) & ((c * KC + row) < cand))
                return jnp.where(cnt < need, cand, x)
            x = lax.fori_loop(0, 12, idx_step, jnp.zeros((1, QB), I32))
            cidx_ref[...] = jnp.broadcast_to(x, cidx_ref.shape)

    tau_ref[...] = jnp.full(tau_ref.shape, INT_MIN, I32)
    cidx_ref[...] = jnp.full(cidx_ref.shape, seq, I32)
    for n in range(2, seq // KC + 1):
        pl.when(qi == n - 1)(functools.partial(find_threshold, n))

    tau = tau_ref[0:1, :]
    cidx = cidx_ref[0:1, :]

    def mask_chunk(c, diag):
        start = pl.multiple_of(c * KC, KC)
        keys = key_ref[pl.ds(start, KC), :]
        keep = (keys > tau) | ((keys == tau) & ((c * KC + row) <= cidx))
        if diag:
            keep = keep & causal_diag
        mb_ref[pl.ds(start, KC), :] = jnp.where(keep, 0.0, NEG_BIG)

    mask_chunk(qi, True)

    def _mask_body(c, carry):
        mask_chunk(c, False)
        return carry

    lax.fori_loop(0, qi, _mask_body, 0)

    _masked_heads(q_ref, qh_ref, H_DSA)

    def chunk_bias(c, diag):
        start = pl.multiple_of(c * KC, KC)
        keep_bias = mb_ref[pl.ds(start, KC), :]
        dist = key_minus_query if diag else key_minus_query + ((c - qi) * KC).astype(F32)
        return lambda head: SLOPES_DSA[head] * dist + keep_bias

    _softmax_heads(qh_ref, k_ref, v_ref, acc_ref, o_ref, qi, H_DSA, chunk_bias)


def _dsa_attention(proj3):
    bsz, seq, _ = proj3.shape
    top = min(DSA_TOPK_MAX, seq // 4)
    assert top == QB, "the first query block must keep every admissible key"
    return pl.pallas_call(
        functools.partial(_dsa_kernel, top=top),
        grid=(bsz, seq // QB),
        in_specs=[
            pl.BlockSpec((None, QB, PAIR_W), lambda b, i: (b, i, 6)),
            pl.BlockSpec((None, seq, PAIR_W), lambda b, i: (b, 0, 7)),
            pl.BlockSpec((None, seq, PAIR_W), lambda b, i: (b, 0, 8)),
            pl.BlockSpec((None, QB, IDX_HEADS * IDX_DIM), lambda b, i: (b, i, COL_QX // (IDX_HEADS * IDX_DIM))),
            pl.BlockSpec((None, QB, LANES), lambda b, i: (b, i, COL_KX // LANES)),
            pl.BlockSpec((None, seq, LANES), lambda b, i: (b, 0, COL_KX // LANES)),
        ],
        out_specs=pl.BlockSpec((None, PAIR_W, QB), lambda b, i: (b, 0, i)),
        out_shape=jax.ShapeDtypeStruct((bsz, PAIR_W, seq), BF16),
        scratch_shapes=[pltpu.VMEM((seq, LANES), BF16),
                        pltpu.VMEM((seq, QB), I32),
                        pltpu.VMEM((seq, QB), I16),
                        pltpu.VMEM((seq, QB), I16),
                        pltpu.VMEM((seq, QB), F32),
                        pltpu.VMEM((8, QB), I32),
                        pltpu.VMEM((8, QB), I32),
                        pltpu.VMEM((H_DSA, QB, LANES), BF16),
                        pltpu.VMEM((PAIR_W // LANES, LANES, QB), F32)],
        compiler_params=pltpu.CompilerParams(
            dimension_semantics=("parallel", "arbitrary"),
            vmem_limit_bytes=48 * 1024 * 1024),
        name="dsa_attn",
    )(proj3, proj3, proj3, proj3, proj3, proj3)


def _merge_kernel(osb_ref, omb_ref, ods_ref, gsb_ref, gmb_ref, gds_ref, x_ref, mod_ref, gpost_ref,
                  wsb_ref, wmb_ref, wds_ref, wo_ref, out_ref):
    def branch(o_ref, w_ref, g_ref):
        y = lax.dot_general(o_ref[...], w_ref[...], TN, preferred_element_type=F32)
        return g_ref[...].astype(F32) * y

    merged = branch(osb_ref, wsb_ref, gsb_ref) + branch(omb_ref, wmb_ref, gmb_ref) + branch(ods_ref, wds_ref, gds_ref)
    y = jnp.dot(merged.astype(BF16), wo_ref[...], preferred_element_type=F32)
    out_ref[...] = x_ref[...] + mod_ref[2:3, :] * (_rms(y) * gpost_ref[...])


def _merge(o_sb, o_mb, o_ds, proj, x2d, mod_l, g_post, w_sb, w_mb, w_ds, w_o, seq):
    t, d = x2d.shape
    tm = 1024
    per_b = seq // tm
    gate_blk = COL_GATE // d
    o_spec = pl.BlockSpec((None, PAIR_W, tm), lambda i: (i // per_b, 0, i % per_b))
    w_spec = pl.BlockSpec((PAIR_W, d), lambda i: (0, 0))
    return pl.pallas_call(
        _merge_kernel,
        grid=(t // tm,),
        in_specs=[
            o_spec, o_spec, o_spec,
            pl.BlockSpec((tm, d), lambda i: (i, gate_blk)),
            pl.BlockSpec((tm, d), lambda i: (i, gate_blk + 1)),
            pl.BlockSpec((tm, d), lambda i: (i, gate_blk + 2)),
            pl.BlockSpec((tm, d), lambda i: (i, 0)),
            pl.BlockSpec((None, 6, d), lambda i: (i // per_b, 0, 0)),
            pl.BlockSpec((1, d), lambda i: (0, 0)),
            w_spec, w_spec, w_spec,
            pl.BlockSpec((d, d), lambda i: (0, 0)),
        ],
        out_specs=pl.BlockSpec((tm, d), lambda i: (i, 0)),
        out_shape=jax.ShapeDtypeStruct((t, d), F32),
        compiler_params=pltpu.CompilerParams(
            dimension_semantics=("parallel",),
            vmem_limit_bytes=48 * 1024 * 1024),
        name="merge_out",
    )(o_sb, o_mb, o_ds, proj, proj, proj, x2d, mod_l, g_post, w_sb, w_mb, w_ds, w_o)


def _ffn_kernel(x_ref, mod_ref, gpre_ref, gpost_ref, wg_ref, wu_ref, wd_ref, out_ref, h_ref, acc_ref):
    xh = _rms(x_ref[...])
    h_ref[...] = ((xh * gpre_ref[...]) * (1.0 + mod_ref[4:5, :]) + mod_ref[3:4, :]).astype(BF16)
    for f in range(0, D_FF, FFN_TF):
        h = h_ref[...]
        gate = jnp.dot(h, wg_ref[:, f:f + FFN_TF], preferred_element_type=F32)
        up = jnp.dot(h, wu_ref[:, f:f + FFN_TF], preferred_element_type=F32)
        act = ((gate * jax.nn.sigmoid(gate)) * up).astype(BF16)
        part = jnp.dot(act, wd_ref[f:f + FFN_TF, :], preferred_element_type=F32)
        if f == 0:
            acc_ref[...] = part
        else:
            acc_ref[...] += part
    y = acc_ref[...]
    out_ref[...] = x_ref[...] + mod_ref[5:6, :] * (_rms(y) * gpost_ref[...])


def _ffn(x2d, mod_l, g_pre, g_post, w_up, w_down, seq):
    t, d = x2d.shape
    tm = 512
    per_b = seq // tm
    return pl.pallas_call(
        _ffn_kernel,
        grid=(t // tm,),
        in_specs=[
            pl.BlockSpec((tm, d), lambda i: (i, 0)),
            pl.BlockSpec((None, 6, d), lambda i: (i // per_b, 0, 0)),
            pl.BlockSpec((1, d), lambda i: (0, 0)),
            pl.BlockSpec((1, d), lambda i: (0, 0)),
            pl.BlockSpec((d, D_FF), lambda i: (0, 0)),
            pl.BlockSpec((d, D_FF), lambda i: (0, 1)),
            pl.BlockSpec((D_FF, d), lambda i: (0, 0)),
        ],
        out_specs=pl.BlockSpec((tm, d), lambda i: (i, 0)),
        out_shape=jax.ShapeDtypeStruct((t, d), F32),
        scratch_shapes=[pltpu.VMEM((tm, d), BF16), pltpu.VMEM((tm, d), F32)],
        compiler_params=pltpu.CompilerParams(
            dimension_semantics=("parallel",),
            vmem_limit_bytes=56 * 1024 * 1024),
        name="ffn",
    )(x2d, mod_l, g_pre, g_post, w_up, w_up, w_down)


def _pad_cols(w, n):
    return jnp.pad(w, ((0, 0), (0, n - w.shape[1])))


def _pad_rows(w, n):
    return jnp.pad(w, ((0, n - w.shape[0]), (0, 0)))


def _layout_w_in(w_in_l, b_gate_l):
    sizes = [W_SB] * 3 + [W_MOBA] * 3 + [W_DSA] * 3 + [IDX_HEADS * IDX_DIM, IDX_DIM, IDX_HEADS, N_BRANCH * D_MODEL]
    points = [int(v) for v in np.cumsum(sizes)[:-1]]
    (q_sb, k_sb, v_sb, q_mb, k_mb, v_mb, q_ds, k_ds, v_ds, q_ix, k_ix, w_ix, gates) = jnp.split(w_in_l, points, axis=1)
    scale = HEAD_DIM ** -0.5 * LOG2E
    cols = [q_sb * scale, k_sb, v_sb,
            _pad_cols(q_mb * scale, PAIR_W), _pad_cols(k_mb, PAIR_W), _pad_cols(v_mb, PAIR_W),
            _pad_cols(q_ds * scale, PAIR_W), _pad_cols(k_ds, PAIR_W), _pad_cols(v_ds, PAIR_W),
            _pad_cols(jnp.concatenate([k_ix, w_ix], axis=1), LANES), q_ix, gates]
    w_all = jnp.concatenate(cols, axis=1).astype(BF16)
    bias = jnp.concatenate([jnp.zeros((COL_GATE,), F32), b_gate_l])[None, :]
    return w_all, bias


def kernel(x, c, w_ada, b_ada, g_pre_mix, g_post_mix, w_in, b_gate, w_proj_sb, w_proj_moba,
           w_proj_dsa, w_o, g_pre_ffn, g_post_ffn, w_up, w_down):
    bsz, seq, d = x.shape
    depth = w_ada.shape[0]
    assert d == D_MODEL and seq % QB == 0 and QB == MOBA_BLOCK and KC == QB
    mod = _ada(c, w_ada, b_ada).reshape(depth, bsz, 6, d)
    x2d = x.reshape(bsz * seq, d)
    for l in range(depth):
        w_all, bias = _layout_w_in(w_in[l], b_gate[l])
        assert w_all.shape[1] == N_PROJ
        proj = _inproj(x2d, mod[l], g_pre_mix[l][None, :], w_all, bias, seq)
        proj3 = proj.reshape(bsz, seq, N_PROJ)
        o_sb = _sb_attention(proj3)
        o_mb = _moba_attention(proj3)
        o_ds = _dsa_attention(proj3)
        x2d = _merge(o_sb, o_mb, o_ds, proj, x2d, mod[l], g_post_mix[l][None, :],
                     w_proj_sb[l].astype(BF16),
                     _pad_rows(w_proj_moba[l], PAIR_W).astype(BF16),
                     _pad_rows(w_proj_dsa[l], PAIR_W).astype(BF16),
                     w_o[l].astype(BF16), seq)
        x2d = _ffn(x2d, mod[l], g_pre_ffn[l][None, :], g_post_ffn[l][None, :],
                   w_up[l].astype(BF16), w_down[l].astype(BF16), seq)
    return x2d.reshape(bsz, seq, d)
```

```python
import functools

import numpy as np
import jax
import jax.numpy as jnp
from jax import lax
from jax.experimental import pallas as pl
from jax.experimental.pallas import tpu as pltpu

F32 = jnp.float32
BF16 = jnp.bfloat16
I32 = jnp.int32
I16 = jnp.int16

D_MODEL = 1024
HEAD_DIM = 64
H_SB, H_MOBA, H_DSA = 6, 5, 5
W_SB, W_MOBA, W_DSA = H_SB * HEAD_DIM, H_MOBA * HEAD_DIM, H_DSA * HEAD_DIM
MOBA_BLOCK = 256
MOBA_TOPK = 3
DSA_TOPK_MAX = 256
IDX_HEADS = 8
IDX_DIM = 64
D_FF = 2816
N_BRANCH = 3
RMS_EPS = 1e-6
NEG_BIG = -1e30
ALIBI_HEADS = H_MOBA + H_DSA

LANES = 128
PAIR_W = 3 * LANES
QB = 256
KC = 256
INT_MIN = -(2 ** 31)
I16_MIN = -(2 ** 15)
PACK16 = 16

COL_KX = 9 * PAIR_W
COL_QX = COL_KX + LANES
COL_GATE = COL_QX + IDX_HEADS * IDX_DIM
N_PROJ = COL_GATE + N_BRANCH * D_MODEL
PROJ_TN = 1024
FFN_TF = 256

NT = (((1,), (1,)), ((), ()))
TN = (((0,), (0,)), ((), ()))

LOG2E = 1.4426950408889634
_ALIBI = [float(2.0 ** (-8.0 * h / ALIBI_HEADS)) * LOG2E for h in range(1, ALIBI_HEADS + 1)]
SLOPES_MOBA = _ALIBI[0::2]
SLOPES_DSA = _ALIBI[1::2]


def _rms(x):
    return x * lax.rsqrt(jnp.mean(x * x, axis=-1, keepdims=True) + RMS_EPS)


def _fori_by_two(n, body, init):
    def two(i, state):
        return body(2 * i + 1, body(2 * i, state))
    state = lax.fori_loop(0, n // 2, two, init)
    return lax.fori_loop(2 * (n // 2), n, body, state)


def _half_mask(shape, lane_axis, hh):
    lane = lax.broadcasted_iota(I32, shape, lane_axis)
    return (lane < HEAD_DIM) if hh == 0 else (lane >= HEAD_DIM)


def _ada_kernel(c_ref, w_ref, b_ref, o_ref):
    c = c_ref[...]
    ca = c * jax.nn.sigmoid(c)
    o_ref[...] = jnp.dot(ca, w_ref[...], preferred_element_type=F32) + b_ref[...]


def _ada(c, w_ada, b_ada):
    depth, d, n = w_ada.shape
    bsz = c.shape[0]
    tn = 512
    return pl.pallas_call(
        _ada_kernel,
        grid=(depth, n // tn),
        in_specs=[
            pl.BlockSpec((bsz, d), lambda l, j: (0, 0)),
            pl.BlockSpec((None, d, tn), lambda l, j: (l, 0, j)),
            pl.BlockSpec((None, 1, tn), lambda l, j: (l, 0, j)),
        ],
        out_specs=pl.BlockSpec((None, bsz, tn), lambda l, j: (l, 0, j)),
        out_shape=jax.ShapeDtypeStruct((depth, bsz, n), F32),
        name="ada_mod",
    )(c, w_ada, b_ada.reshape(depth, 1, n))


def _inproj_kernel(x_ref, mod_ref, g_ref, w_ref, b_ref, o_ref):
    xh = _rms(x_ref[...])
    h = ((xh * g_ref[...]) * (1.0 + mod_ref[1:2, :]) + mod_ref[0:1, :]).astype(BF16)
    for n0 in range(0, N_PROJ, PROJ_TN):
        cols = slice(n0, n0 + PROJ_TN)
        acc = jnp.dot(h, w_ref[:, cols], preferred_element_type=F32) + b_ref[:, cols]
        if n0 >= COL_GATE:
            acc = jax.nn.sigmoid(acc)
        o_ref[:, cols] = acc.astype(BF16)


def _inproj(x2d, mod, g_pre, w_all, bias, layer, seq):
    t, d = x2d.shape
    tm = 512
    n = w_all.shape[-1]
    per_b = seq // tm
    return pl.pallas_call(
        _inproj_kernel,
        grid=(t // tm,),
        in_specs=[
            pl.BlockSpec((tm, d), lambda i: (i, 0)),
            pl.BlockSpec((None, None, 6, d), lambda i: (layer, i // per_b, 0, 0)),
            pl.BlockSpec((None, 1, d), lambda i: (layer, 0, 0)),
            pl.BlockSpec((None, d, n), lambda i: (layer, 0, 0)),
            pl.BlockSpec((None, 1, n), lambda i: (layer, 0, 0)),
        ],
        out_specs=pl.BlockSpec((tm, n), lambda i: (i, 0)),
        out_shape=jax.ShapeDtypeStruct((t, n), BF16),
        compiler_params=pltpu.CompilerParams(
            dimension_semantics=("parallel",),
            vmem_limit_bytes=56 * 1024 * 1024),
        name="in_proj",
    )(x2d, mod, g_pre, w_all, bias)


def _masked_heads(ref, dst_ref, n_heads):
    for head in range(n_heads):
        grp = ref[:, (head // 2) * LANES:(head // 2 + 1) * LANES]
        dst_ref[head] = jnp.where(_half_mask(grp.shape, 1, head % 2), grp, jnp.zeros_like(grp))


def _sb_kernel(q_ref, k_ref, v_ref, o_ref, qh_ref, lb_ref, l1m_ref, carry_ref, acc_ref):
    qi = pl.program_id(1)
    row = lax.broadcasted_iota(I32, (KC, QB), 0)
    lane = lax.broadcasted_iota(I32, (KC, QB), 1)
    past_diag = row < lane
    upper = (lax.broadcasted_iota(I32, (KC, KC), 1) > lax.broadcasted_iota(I32, (KC, KC), 0)).astype(BF16)

    _masked_heads(q_ref, qh_ref, H_SB)

    def terms(c, carries, diag):
        start = pl.multiple_of(c * KC, KC)
        new = list(carries)
        for p in range(H_SB // 2):
            kc = k_ref[pl.ds(start, KC), p * LANES:(p + 1) * LANES]
            for head in (2 * p, 2 * p + 1):
                z = lax.dot_general(kc, qh_ref[head], NT, preferred_element_type=F32)
                neg_abs_z = pltpu.bitcast(pltpu.bitcast(z, I32) | jnp.int32(INT_MIN), F32)
                log_beta = jnp.minimum(z, 0.0) - jnp.log2(1.0 + jnp.exp2(neg_abs_z))
                log_1m = log_beta - z
                if diag:
                    log_1m = jnp.where(past_diag, log_1m, 0.0)
                    log_beta = jnp.where(past_diag, log_beta, NEG_BIG)
                lb_ref[head, c] = log_beta
                l1m_ref[head, c] = log_1m.astype(BF16)
                carry_ref[head, c] = carries[head]
                new[head] = carries[head] + _fold8(log_1m, jnp.sum)
        return tuple(new)

    carries = terms(qi, tuple(jnp.zeros((8, QB), F32) for _ in range(H_SB)), True)
    _fori_by_two(qi, lambda i, cs: terms(qi - 1 - i, cs, False), carries)

    acc_ref[...] = jnp.zeros_like(acc_ref)

    def attend(c, token):
        start = pl.multiple_of(c * KC, KC)
        for p in range(H_SB // 2):
            vc = v_ref[pl.ds(start, KC), p * LANES:(p + 1) * LANES]
            pv = None
            for head in (2 * p, 2 * p + 1):
                between = jnp.dot(upper, l1m_ref[head, c], preferred_element_type=F32)
                carry = jnp.sum(carry_ref[head, c], axis=0, keepdims=True)
                a = jnp.exp2(lb_ref[head, c] + between + carry)
                vh = jnp.where(_half_mask(vc.shape, 1, head % 2), vc, jnp.zeros_like(vc))
                part = lax.dot_general(vh, a.astype(BF16), TN, preferred_element_type=F32)
                pv = part if pv is None else pv + part
            acc_ref[p] += pv
        return token

    _fori_by_two(qi + 1, attend, 0)
    for p in range(H_SB // 2):
        o_ref[p * LANES:(p + 1) * LANES, :] = acc_ref[p].astype(BF16)


def _sb_attention(proj3):
    bsz, seq, _ = proj3.shape
    return pl.pallas_call(
        _sb_kernel,
        grid=(bsz, seq // QB),
        in_specs=[
            pl.BlockSpec((None, QB, PAIR_W), lambda b, i: (b, i, 0)),
            pl.BlockSpec((None, seq, PAIR_W), lambda b, i: (b, 0, 1)),
            pl.BlockSpec((None, seq, PAIR_W), lambda b, i: (b, 0, 2)),
        ],
        out_specs=pl.BlockSpec((None, PAIR_W, QB), lambda b, i: (b, 0, i)),
        out_shape=jax.ShapeDtypeStruct((bsz, PAIR_W, seq), BF16),
        scratch_shapes=[pltpu.VMEM((H_SB, QB, LANES), BF16),
                        pltpu.VMEM((H_SB, seq // KC, KC, QB), F32),
                        pltpu.VMEM((H_SB, seq // KC, KC, QB), BF16),
                        pltpu.VMEM((H_SB, seq // KC, 8, QB), F32),
                        pltpu.VMEM((H_SB // 2, LANES, QB), F32)],
        compiler_params=pltpu.CompilerParams(
            dimension_semantics=("parallel", "arbitrary"),
            vmem_limit_bytes=48 * 1024 * 1024),
        name="sb_attn",
    )(proj3, proj3, proj3)


def _fold8(x, op):
    return op(x.reshape(KC // 8, 8, QB), axis=0)


def _softmax_heads(qh_ref, k_ref, v_ref, s_ref, acc_ref, o_ref, qi, n_heads, chunk_bias):
    out_row = lax.broadcasted_iota(I32, (LANES, QB), 0)
    n_pairs = (n_heads + 1) // 2
    pair_heads = [[h for h in (2 * p, 2 * p + 1) if h < n_heads] for p in range(n_pairs)]

    def by_head_rows(vals):
        return vals[0] if len(vals) == 1 else jnp.where(out_row < HEAD_DIM, vals[0], vals[1])

    def score(c, maxes, diag):
        start = pl.multiple_of(c * KC, KC)
        bias_of = chunk_bias(c, diag)
        new = list(maxes)
        for p in range(n_pairs):
            kc = k_ref[pl.ds(start, KC), p * LANES:(p + 1) * LANES]
            for head in pair_heads[p]:
                s = lax.dot_general(kc, qh_ref[head], NT, preferred_element_type=F32) + bias_of(head)
                s_ref[head, c] = s
                new[head] = jnp.maximum(maxes[head], _fold8(s, jnp.max))
        return tuple(new)

    maxes = score(qi, tuple(jnp.full((8, QB), NEG_BIG, F32) for _ in range(n_heads)), True)
    maxes = _fori_by_two(qi, lambda c, mx: score(c, mx, False), maxes)
    m = [jnp.max(mx, axis=0, keepdims=True) for mx in maxes]

    acc_ref[...] = jnp.zeros_like(acc_ref)

    def attend(c, sums):
        start = pl.multiple_of(c * KC, KC)
        new = list(sums)
        for p in range(n_pairs):
            vc = v_ref[pl.ds(start, KC), p * LANES:(p + 1) * LANES]
            pv = None
            for head in pair_heads[p]:
                pr = jnp.exp2(s_ref[head, c] - m[head])
                new[head] = sums[head] + _fold8(pr, jnp.sum)
                vh = vc if len(pair_heads[p]) == 1 else jnp.where(
                    _half_mask(vc.shape, 1, head % 2), vc, jnp.zeros_like(vc))
                part = lax.dot_general(vh, pr.astype(BF16), TN, preferred_element_type=F32)
                pv = part if pv is None else pv + part
            acc_ref[p] += pv
        return tuple(new)

    sums = _fori_by_two(qi + 1, attend, tuple(jnp.zeros((8, QB), F32) for _ in range(n_heads)))
    for p in range(n_pairs):
        denom = by_head_rows([jnp.sum(sums[h], axis=0, keepdims=True) for h in pair_heads[p]])
        o_ref[p * LANES:(p + 1) * LANES, :] = (acc_ref[p] / denom).astype(BF16)


def _moba_kernel(q_ref, k_ref, v_ref, o_ref, kmean_ref, sel_ref, qh_ref, s_ref, acc_ref):
    qi = pl.program_id(1)
    n_blocks = kmean_ref.shape[0]

    @pl.when(qi == 0)
    def _():
        for n in range(n_blocks):
            kb = k_ref[n * MOBA_BLOCK:(n + 1) * MOBA_BLOCK, :].astype(F32)
            kmean_ref[n:n + 1, :] = jnp.mean(kb, axis=0, keepdims=True)

    row = lax.broadcasted_iota(I32, (KC, QB), 0)
    lane = lax.broadcasted_iota(I32, (KC, QB), 1)
    causal_diag = row <= lane
    key_minus_query = (row - lane).astype(F32)
    blk = lax.broadcasted_iota(I32, (n_blocks, QB), 0)

    _masked_heads(q_ref, qh_ref, H_MOBA)

    for head in range(H_MOBA):
        cols = slice((head // 2) * LANES, (head // 2 + 1) * LANES)
        gate = lax.dot_general(kmean_ref[:, cols], qh_ref[head].astype(F32), NT,
                               preferred_element_type=F32, precision=lax.Precision.HIGHEST)
        rank = jnp.zeros((n_blocks, QB), I32)
        for mb in range(n_blocks):
            gm = gate[mb:mb + 1, :]
            beats = (gm > gate) | ((gm == gate) & (mb < blk))
            rank = rank + jnp.where(beats, (mb < qi).astype(I32), 0)
        selected = (rank < MOBA_TOPK) & (blk < qi)
        sel_ref[head] = jnp.where(selected, 0.0, NEG_BIG)

    def chunk_bias(c, diag):
        if diag:
            return lambda head: jnp.where(causal_diag, SLOPES_MOBA[head] * key_minus_query, NEG_BIG)
        dist = key_minus_query + ((c - qi) * KC).astype(F32)
        return lambda head: SLOPES_MOBA[head] * dist + sel_ref[head, pl.ds(c, 1), :]

    _softmax_heads(qh_ref, k_ref, v_ref, s_ref, acc_ref, o_ref, qi, H_MOBA, chunk_bias)


def _moba_attention(proj3):
    bsz, seq, _ = proj3.shape
    n_blocks = seq // MOBA_BLOCK
    return pl.pallas_call(
        _moba_kernel,
        grid=(bsz, seq // QB),
        in_specs=[
            pl.BlockSpec((None, QB, PAIR_W), lambda b, i: (b, i, 3)),
            pl.BlockSpec((None, seq, PAIR_W), lambda b, i: (b, 0, 4)),
            pl.BlockSpec((None, seq, PAIR_W), lambda b, i: (b, 0, 5)),
        ],
        out_specs=pl.BlockSpec((None, PAIR_W, QB), lambda b, i: (b, 0, i)),
        out_shape=jax.ShapeDtypeStruct((bsz, PAIR_W, seq), BF16),
        scratch_shapes=[pltpu.VMEM((n_blocks, PAIR_W), F32),
                        pltpu.VMEM((H_MOBA, n_blocks, QB), F32),
                        pltpu.VMEM((H_MOBA, QB, LANES), BF16),
                        pltpu.VMEM((H_MOBA, seq // KC, KC, QB), F32),
                        pltpu.VMEM((PAIR_W // LANES, LANES, QB), F32)],
        compiler_params=pltpu.CompilerParams(
            dimension_semantics=("parallel", "arbitrary"),
            vmem_limit_bytes=48 * 1024 * 1024),
        name="moba_attn",
    )(proj3, proj3, proj3)


def _dsa_kernel(q_ref, k_ref, v_ref, qx_ref, kxq_ref, kx_ref, o_ref,
                kx2_ref, key_ref, hi_ref, lo_ref, mb_ref, tau_ref, cidx_ref, qh_ref, s_ref, acc_ref, *, top):
    qi = pl.program_id(1)
    seq = k_ref.shape[0]

    @pl.when(qi == 0)
    def _():
        kx = kx_ref[...].astype(F32)
        dup = jnp.where(_half_mask(kx.shape, 1, 0), kx, pltpu.roll(kx, HEAD_DIM, 1))
        kx2_ref[...] = dup.astype(BF16)

    row = lax.broadcasted_iota(I32, (KC, QB), 0)
    lane = lax.broadcasted_iota(I32, (KC, QB), 1)
    causal_diag = row <= lane
    key_minus_query = (row - lane).astype(F32)

    pick = (lax.broadcasted_iota(I32, (IDX_HEADS, LANES), 1)
            == lax.broadcasted_iota(I32, (IDX_HEADS, LANES), 0) + IDX_DIM).astype(BF16)
    w_t = lax.dot_general(pick, kxq_ref[...], NT, preferred_element_type=F32)

    def score_chunk(c, diag):
        start = pl.multiple_of(c * KC, KC)
        kc = kx2_ref[pl.ds(start, KC), :]
        sc = jnp.zeros((KC, QB), F32)
        for pp in range(IDX_HEADS // 2):
            qp = qx_ref[:, pp * LANES:(pp + 1) * LANES]
            for hh in range(2):
                h = 2 * pp + hh
                qh = jnp.where(_half_mask(qp.shape, 1, hh), qp, jnp.zeros_like(qp))
                lg = lax.dot_general(kc, qh, NT, preferred_element_type=F32)
                sc = sc + jnp.maximum(lg, 0.0) * w_t[h:h + 1, :]
        if diag:
            sc = jnp.where(causal_diag, sc, NEG_BIG)
        bits = pltpu.bitcast(sc, I32)
        key = jnp.where(bits < 0, bits ^ jnp.int32(0x7FFFFFFF), bits)
        key_ref[pl.ds(start, KC), :] = key
        hi_ref[pl.ds(start, KC), :] = lax.shift_right_arithmetic(key, 16).astype(I16)
        lo_ref[pl.ds(start, KC), :] = ((key & 0xFFFF) + I16_MIN).astype(I16)

    score_chunk(qi, True)

    def _score_body(c, carry):
        score_chunk(c, False)
        return carry

    lax.fori_loop(0, qi, _score_body, 0)

    shape16 = (KC // PACK16, PACK16, QB)

    def count(n, pred):
        acc8 = jnp.zeros((8, QB), I32)
        for c in range(n):
            hit = jnp.where(pred(key_ref[c * KC:(c + 1) * KC, :], c), 1, 0)
            acc8 = acc8 + jnp.sum(hit.reshape(KC // 8, 8, QB), axis=0)
        return jnp.sum(acc8, axis=0, keepdims=True)

    def packed_rows(x):
        return jnp.broadcast_to(x, (PACK16, QB)).astype(I16)[None]

    def count16(ref, n, pred):
        acc = jnp.zeros((PACK16, QB), I16)
        for c in range(n):
            hit = jnp.where(pred(ref[c * KC:(c + 1) * KC, :].reshape(shape16)), jnp.int16(1), jnp.int16(0))
            for r in range(KC // PACK16):
                acc = acc + hit[r]
        return jnp.sum(acc.astype(I32), axis=0, keepdims=True)

    def kth_largest16(ref, n, kth):
        def bit_step(i, tau):
            cand = tau + lax.shift_left(jnp.int32(1), 15 - i)
            cand16 = packed_rows(cand)
            cnt = count16(ref, n, lambda v: v >= cand16)
            return jnp.where(cnt >= kth, cand, tau)
        return lax.fori_loop(0, 16, bit_step, jnp.full((1, QB), I16_MIN, I32))

    def find_threshold(n):
        tau_hi = kth_largest16(hi_ref, n, top)
        tau_hi16 = packed_rows(tau_hi)
        above = count16(hi_ref, n, lambda v: v > tau_hi16)
        for c in range(n):
            rows = slice(c * KC, (c + 1) * KC)
            parked = jnp.where(hi_ref[rows, :].reshape(shape16) == tau_hi16,
                               lo_ref[rows, :].reshape(shape16), jnp.int16(I16_MIN))
            lo_ref[rows, :] = parked.reshape(KC, QB)
        tau_lo = kth_largest16(lo_ref, n, top - above)
        tau = lax.shift_left(tau_hi, 16) | (tau_lo - I16_MIN)
        tau_ref[...] = jnp.broadcast_to(tau, tau_ref.shape)

        cnt_ge = count(n, lambda keys, c: keys >= tau)
        cnt_gt = count(n, lambda keys, c: keys > tau)
        need = top - cnt_gt

        @pl.when(jnp.max(cnt_ge) > top)
        def _():
            def idx_step(i, x):
                cand = x + lax.shift_left(jnp.int32(1), 11 - i)
                cnt = count(n, lambda keys, c: (keys == tau) & ((c * KC + row) < cand))
                return jnp.where(cnt < need, cand, x)
            x = lax.fori_loop(0, 12, idx_step, jnp.zeros((1, QB), I32))
            cidx_ref[...] = jnp.broadcast_to(x, cidx_ref.shape)

    tau_ref[...] = jnp.full(tau_ref.shape, INT_MIN, I32)
    cidx_ref[...] = jnp.full(cidx_ref.shape, seq, I32)
    for n in range(2, seq // KC + 1):
        pl.when(qi == n - 1)(functools.partial(find_threshold, n))

    tau = tau_ref[0:1, :]
    cidx = cidx_ref[0:1, :]

    def mask_chunk(c, diag):
        start = pl.multiple_of(c * KC, KC)
        keys = key_ref[pl.ds(start, KC), :]
        keep = (keys > tau) | ((keys == tau) & ((c * KC + row) <= cidx))
        if diag:
            keep = keep & causal_diag
        mb_ref[pl.ds(start, KC), :] = jnp.where(keep, 0.0, NEG_BIG)

    mask_chunk(qi, True)

    def _mask_body(c, carry):
        mask_chunk(c, False)
        return carry

    lax.fori_loop(0, qi, _mask_body, 0)

    _masked_heads(q_ref, qh_ref, H_DSA)

    def chunk_bias(c, diag):
        start = pl.multiple_of(c * KC, KC)
        keep_bias = mb_ref[pl.ds(start, KC), :]
        dist = key_minus_query if diag else key_minus_query + ((c - qi) * KC).astype(F32)
        return lambda head: SLOPES_DSA[head] * dist + keep_bias

    _softmax_heads(qh_ref, k_ref, v_ref, s_ref, acc_ref, o_ref, qi, H_DSA, chunk_bias)


def _dsa_attention(proj3):
    bsz, seq, _ = proj3.shape
    top = min(DSA_TOPK_MAX, seq // 4)
    assert top == QB, "the first query block must keep every admissible key"
    return pl.pallas_call(
        functools.partial(_dsa_kernel, top=top),
        grid=(bsz, seq // QB),
        in_specs=[
            pl.BlockSpec((None, QB, PAIR_W), lambda b, i: (b, i, 6)),
            pl.BlockSpec((None, seq, PAIR_W), lambda b, i: (b, 0, 7)),
            pl.BlockSpec((None, seq, PAIR_W), lambda b, i: (b, 0, 8)),
            pl.BlockSpec((None, QB, IDX_HEADS * IDX_DIM), lambda b, i: (b, i, COL_QX // (IDX_HEADS * IDX_DIM))),
            pl.BlockSpec((None, QB, LANES), lambda b, i: (b, i, COL_KX // LANES)),
            pl.BlockSpec((None, seq, LANES), lambda b, i: (b, 0, COL_KX // LANES)),
        ],
        out_specs=pl.BlockSpec((None, PAIR_W, QB), lambda b, i: (b, 0, i)),
        out_shape=jax.ShapeDtypeStruct((bsz, PAIR_W, seq), BF16),
        scratch_shapes=[pltpu.VMEM((seq, LANES), BF16),
                        pltpu.VMEM((seq, QB), I32),
                        pltpu.VMEM((seq, QB), I16),
                        pltpu.VMEM((seq, QB), I16),
                        pltpu.VMEM((seq, QB), F32),
                        pltpu.VMEM((8, QB), I32),
                        pltpu.VMEM((8, QB), I32),
                        pltpu.VMEM((H_DSA, QB, LANES), BF16),
                        pltpu.VMEM((H_DSA, seq // KC, KC, QB), F32),
                        pltpu.VMEM((PAIR_W // LANES, LANES, QB), F32)],
        compiler_params=pltpu.CompilerParams(
            dimension_semantics=("parallel", "arbitrary"),
            vmem_limit_bytes=48 * 1024 * 1024),
        name="dsa_attn",
    )(proj3, proj3, proj3, proj3, proj3, proj3)


def _merge_kernel(osb_ref, omb_ref, ods_ref, gsb_ref, gmb_ref, gds_ref, x_ref, mod_ref, gpost_ref,
                  wsb_ref, wmb_ref, wds_ref, wo_ref, out_ref):
    def branch(o_ref, w_ref, g_ref):
        y = lax.dot_general(o_ref[...], w_ref[...], TN, preferred_element_type=F32)
        return g_ref[...].astype(F32) * y

    merged = branch(osb_ref, wsb_ref, gsb_ref) + branch(omb_ref, wmb_ref, gmb_ref) + branch(ods_ref, wds_ref, gds_ref)
    y = jnp.dot(merged.astype(BF16), wo_ref[...], preferred_element_type=F32)
    out_ref[...] = x_ref[...] + mod_ref[2:3, :] * (_rms(y) * gpost_ref[...])


def _merge(o_sb, o_mb, o_ds, proj, x2d, mod, g_post, w_sb, w_mb, w_ds, w_o, layer, seq):
    t, d = x2d.shape
    tm = 1024
    per_b = seq // tm
    gate_blk = COL_GATE // d
    o_spec = pl.BlockSpec((None, PAIR_W, tm), lambda i: (i // per_b, 0, i % per_b))
    w_spec = pl.BlockSpec((None, PAIR_W, d), lambda i: (layer, 0, 0))
    return pl.pallas_call(
        _merge_kernel,
        grid=(t // tm,),
        in_specs=[
            o_spec, o_spec, o_spec,
            pl.BlockSpec((tm, d), lambda i: (i, gate_blk)),
            pl.BlockSpec((tm, d), lambda i: (i, gate_blk + 1)),
            pl.BlockSpec((tm, d), lambda i: (i, gate_blk + 2)),
            pl.BlockSpec((tm, d), lambda i: (i, 0)),
            pl.BlockSpec((None, None, 6, d), lambda i: (layer, i // per_b, 0, 0)),
            pl.BlockSpec((None, 1, d), lambda i: (layer, 0, 0)),
            w_spec, w_spec, w_spec,
            pl.BlockSpec((None, d, d), lambda i: (layer, 0, 0)),
        ],
        out_specs=pl.BlockSpec((tm, d), lambda i: (i, 0)),
        out_shape=jax.ShapeDtypeStruct((t, d), F32),
        compiler_params=pltpu.CompilerParams(
            dimension_semantics=("parallel",),
            vmem_limit_bytes=48 * 1024 * 1024),
        name="merge_out",
    )(o_sb, o_mb, o_ds, proj, proj, proj, x2d, mod, g_post, w_sb, w_mb, w_ds, w_o)


def _ffn_kernel(x_ref, mod_ref, gpre_ref, gpost_ref, wg_ref, wu_ref, wd_ref, out_ref, h_ref, acc_ref):
    xh = _rms(x_ref[...])
    h_ref[...] = ((xh * gpre_ref[...]) * (1.0 + mod_ref[4:5, :]) + mod_ref[3:4, :]).astype(BF16)
    for f in range(0, D_FF, FFN_TF):
        h = h_ref[...]
        gate = jnp.dot(h, wg_ref[:, f:f + FFN_TF], preferred_element_type=F32)
        up = jnp.dot(h, wu_ref[:, f:f + FFN_TF], preferred_element_type=F32)
        act = ((gate * jax.nn.sigmoid(gate)) * up).astype(BF16)
        part = jnp.dot(act, wd_ref[f:f + FFN_TF, :], preferred_element_type=F32)
        if f == 0:
            acc_ref[...] = part
        else:
            acc_ref[...] += part
    y = acc_ref[...]
    out_ref[...] = x_ref[...] + mod_ref[5:6, :] * (_rms(y) * gpost_ref[...])


def _ffn(x2d, mod, g_pre, g_post, w_up, w_down, layer, seq):
    t, d = x2d.shape
    tm = 512
    per_b = seq // tm
    return pl.pallas_call(
        _ffn_kernel,
        grid=(t // tm,),
        in_specs=[
            pl.BlockSpec((tm, d), lambda i: (i, 0)),
            pl.BlockSpec((None, None, 6, d), lambda i: (layer, i // per_b, 0, 0)),
            pl.BlockSpec((None, 1, d), lambda i: (layer, 0, 0)),
            pl.BlockSpec((None, 1, d), lambda i: (layer, 0, 0)),
            pl.BlockSpec((None, d, D_FF), lambda i: (layer, 0, 0)),
            pl.BlockSpec((None, d, D_FF), lambda i: (layer, 0, 1)),
            pl.BlockSpec((None, D_FF, d), lambda i: (layer, 0, 0)),
        ],
        out_specs=pl.BlockSpec((tm, d), lambda i: (i, 0)),
        out_shape=jax.ShapeDtypeStruct((t, d), F32),
        scratch_shapes=[pltpu.VMEM((tm, d), BF16), pltpu.VMEM((tm, d), F32)],
        compiler_params=pltpu.CompilerParams(
            dimension_semantics=("parallel",),
            vmem_limit_bytes=56 * 1024 * 1024),
        name="ffn",
    )(x2d, mod, g_pre, g_post, w_up, w_up, w_down)


def _pad_last(w, n):
    return jnp.pad(w, [(0, 0)] * (w.ndim - 1) + [(0, n - w.shape[-1])])


def _pad_rows(w, n):
    return jnp.pad(w, ((0, 0), (0, n - w.shape[1]), (0, 0)))


def _layout_w_in(w_in, b_gate):
    sizes = [W_SB] * 3 + [W_MOBA] * 3 + [W_DSA] * 3 + [IDX_HEADS * IDX_DIM, IDX_DIM, IDX_HEADS, N_BRANCH * D_MODEL]
    points = [int(v) for v in np.cumsum(sizes)[:-1]]
    (q_sb, k_sb, v_sb, q_mb, k_mb, v_mb, q_ds, k_ds, v_ds, q_ix, k_ix, w_ix, gates) = jnp.split(w_in, points, axis=-1)
    scale = HEAD_DIM ** -0.5 * LOG2E
    cols = [q_sb * scale, k_sb, v_sb,
            _pad_last(q_mb * scale, PAIR_W), _pad_last(k_mb, PAIR_W), _pad_last(v_mb, PAIR_W),
            _pad_last(q_ds * scale, PAIR_W), _pad_last(k_ds, PAIR_W), _pad_last(v_ds, PAIR_W),
            _pad_last(jnp.concatenate([k_ix, w_ix], axis=-1), LANES), q_ix, gates]
    w_all = jnp.concatenate(cols, axis=-1).astype(BF16)
    bias = jnp.concatenate([jnp.zeros((b_gate.shape[0], COL_GATE), F32), b_gate], axis=-1)[:, None, :]
    return w_all, bias


def kernel(x, c, w_ada, b_ada, g_pre_mix, g_post_mix, w_in, b_gate, w_proj_sb, w_proj_moba,
           w_proj_dsa, w_o, g_pre_ffn, g_post_ffn, w_up, w_down):
    bsz, seq, d = x.shape
    depth = w_ada.shape[0]
    assert d == D_MODEL and seq % QB == 0 and QB == MOBA_BLOCK and KC == QB
    mod = _ada(c, w_ada, b_ada).reshape(depth, bsz, 6, d)
    w_all, bias = _layout_w_in(w_in, b_gate)
    assert w_all.shape[-1] == N_PROJ
    w_sb = w_proj_sb.astype(BF16)
    w_mb = _pad_rows(w_proj_moba, PAIR_W).astype(BF16)
    w_ds = _pad_rows(w_proj_dsa, PAIR_W).astype(BF16)
    w_o, w_up, w_down = w_o.astype(BF16), w_up.astype(BF16), w_down.astype(BF16)
    rows = lambda g: g[:, None, :]
    x2d = x.reshape(bsz * seq, d)
    for layer in range(depth):
        proj = _inproj(x2d, mod, rows(g_pre_mix), w_all, bias, layer, seq)
        proj3 = proj.reshape(bsz, seq, N_PROJ)
        o_sb = _sb_attention(proj3)
        o_mb = _moba_attention(proj3)
        o_ds = _dsa_attention(proj3)
        x2d = _merge(o_sb, o_mb, o_ds, proj, x2d, mod, rows(g_post_mix), w_sb, w_mb, w_ds, w_o, layer, seq)
        x2d = _ffn(x2d, mod, rows(g_pre_ffn), rows(g_post_ffn), w_up, w_down, layer, seq)
    return x2d.reshape(bsz, seq, d)
```

```python
import functools

import numpy as np
import jax
import jax.numpy as jnp
from jax import lax
from jax.experimental import pallas as pl
from jax.experimental.pallas import tpu as pltpu

F32 = jnp.float32
BF16 = jnp.bfloat16
I32 = jnp.int32
I16 = jnp.int16

D_MODEL = 1024
HEAD_DIM = 64
H_SB, H_MOBA, H_DSA = 6, 5, 5
W_SB, W_MOBA, W_DSA = H_SB * HEAD_DIM, H_MOBA * HEAD_DIM, H_DSA * HEAD_DIM
MOBA_BLOCK = 256
MOBA_TOPK = 3
DSA_TOPK_MAX = 256
IDX_HEADS = 8
IDX_DIM = 64
D_FF = 2816
N_BRANCH = 3
RMS_EPS = 1e-6
NEG_BIG = -1e30
ALIBI_HEADS = H_MOBA + H_DSA

LANES = 128
PAIR_W = 3 * LANES
QB = 256
KC = 256
INT_MIN = -(2 ** 31)
I16_MIN = -(2 ** 15)
PACK16 = 16

COL_KX = 9 * PAIR_W
COL_QX = COL_KX + LANES
COL_GATE = COL_QX + IDX_HEADS * IDX_DIM
N_PROJ = COL_GATE + N_BRANCH * D_MODEL
PROJ_TN = 1024
FFN_TF = 256

NT = (((1,), (1,)), ((), ()))
TN = (((0,), (0,)), ((), ()))

LOG2E = 1.4426950408889634
_ALIBI = [float(2.0 ** (-8.0 * h / ALIBI_HEADS)) * LOG2E for h in range(1, ALIBI_HEADS + 1)]
SLOPES_MOBA = _ALIBI[0::2]
SLOPES_DSA = _ALIBI[1::2]


def _rms(x):
    return x * lax.rsqrt(jnp.mean(x * x, axis=-1, keepdims=True) + RMS_EPS)


def _fori_by_two(n, body, init):
    def two(i, state):
        return body(2 * i + 1, body(2 * i, state))
    state = lax.fori_loop(0, n // 2, two, init)
    return lax.fori_loop(2 * (n // 2), n, body, state)


def _half_mask(shape, lane_axis, hh):
    lane = lax.broadcasted_iota(I32, shape, lane_axis)
    return (lane < HEAD_DIM) if hh == 0 else (lane >= HEAD_DIM)


def _ada_kernel(c_ref, w_ref, b_ref, o_ref):
    c = c_ref[...]
    ca = c * jax.nn.sigmoid(c)
    o_ref[...] = jnp.dot(ca, w_ref[...], preferred_element_type=F32) + b_ref[...]


def _ada(c, w_ada, b_ada):
    depth, d, n = w_ada.shape
    bsz = c.shape[0]
    tn = 512
    return pl.pallas_call(
        _ada_kernel,
        grid=(depth, n // tn),
        in_specs=[
            pl.BlockSpec((bsz, d), lambda l, j: (0, 0)),
            pl.BlockSpec((None, d, tn), lambda l, j: (l, 0, j)),
            pl.BlockSpec((None, 1, tn), lambda l, j: (l, 0, j)),
        ],
        out_specs=pl.BlockSpec((None, bsz, tn), lambda l, j: (l, 0, j)),
        out_shape=jax.ShapeDtypeStruct((depth, bsz, n), F32),
        name="ada_mod",
    )(c, w_ada, b_ada.reshape(depth, 1, n))


def _inproj_kernel(x_ref, mod_ref, g_ref, w_ref, b_ref, o_ref):
    xh = _rms(x_ref[...])
    h = ((xh * g_ref[...]) * (1.0 + mod_ref[1:2, :]) + mod_ref[0:1, :]).astype(BF16)
    for n0 in range(0, N_PROJ, PROJ_TN):
        cols = slice(n0, n0 + PROJ_TN)
        acc = jnp.dot(h, w_ref[:, cols], preferred_element_type=F32) + b_ref[:, cols]
        if n0 >= COL_GATE:
            acc = jax.nn.sigmoid(acc)
        o_ref[:, cols] = acc.astype(BF16)


def _inproj(x2d, mod, g_pre, w_all, bias, layer, seq):
    t, d = x2d.shape
    tm = 512
    n = w_all.shape[-1]
    per_b = seq // tm
    return pl.pallas_call(
        _inproj_kernel,
        grid=(t // tm,),
        in_specs=[
            pl.BlockSpec((tm, d), lambda i: (i, 0)),
            pl.BlockSpec((None, None, 6, d), lambda i: (layer, i // per_b, 0, 0)),
            pl.BlockSpec((None, 1, d), lambda i: (layer, 0, 0)),
            pl.BlockSpec((None, d, n), lambda i: (layer, 0, 0)),
            pl.BlockSpec((None, 1, n), lambda i: (layer, 0, 0)),
        ],
        out_specs=pl.BlockSpec((tm, n), lambda i: (i, 0)),
        out_shape=jax.ShapeDtypeStruct((t, n), BF16),
        compiler_params=pltpu.CompilerParams(
            dimension_semantics=("parallel",),
            vmem_limit_bytes=56 * 1024 * 1024),
        name="in_proj",
    )(x2d, mod, g_pre, w_all, bias)


def _masked_heads(ref, dst_ref, n_heads):
    for head in range(n_heads):
        grp = ref[:, (head // 2) * LANES:(head // 2 + 1) * LANES]
        dst_ref[head] = jnp.where(_half_mask(grp.shape, 1, head % 2), grp, jnp.zeros_like(grp))


def _sb_kernel(q_ref, k_ref, v_ref, o_ref, qh_ref, lb_ref, l1m_ref, carry_ref, acc_ref):
    qi = pl.program_id(1)
    row = lax.broadcasted_iota(I32, (KC, QB), 0)
    lane = lax.broadcasted_iota(I32, (KC, QB), 1)
    past_diag = row < lane
    upper = (lax.broadcasted_iota(I32, (KC, KC), 1) > lax.broadcasted_iota(I32, (KC, KC), 0)).astype(BF16)

    _masked_heads(q_ref, qh_ref, H_SB)

    def terms(c, carries, diag):
        start = pl.multiple_of(c * KC, KC)
        new = list(carries)
        for p in range(H_SB // 2):
            kc = k_ref[pl.ds(start, KC), p * LANES:(p + 1) * LANES]
            for head in (2 * p, 2 * p + 1):
                z = lax.dot_general(kc, qh_ref[head], NT, preferred_element_type=F32)
                neg_abs_z = pltpu.bitcast(pltpu.bitcast(z, I32) | jnp.int32(INT_MIN), F32)
                log_beta = jnp.minimum(z, 0.0) - jnp.log2(1.0 + jnp.exp2(neg_abs_z))
                log_1m = log_beta - z
                if diag:
                    log_1m = jnp.where(past_diag, log_1m, 0.0)
                    log_beta = jnp.where(past_diag, log_beta, NEG_BIG)
                lb_ref[head, c] = log_beta
                l1m_ref[head, c] = log_1m.astype(BF16)
                carry_ref[head, c] = carries[head]
                new[head] = carries[head] + _fold8(log_1m, jnp.sum)
        return tuple(new)

    def attend(c):
        start = pl.multiple_of(c * KC, KC)
        for p in range(H_SB // 2):
            vc = v_ref[pl.ds(start, KC), p * LANES:(p + 1) * LANES]
            pv = None
            for head in (2 * p, 2 * p + 1):
                between = jnp.dot(upper, l1m_ref[head, c], preferred_element_type=F32)
                carry = jnp.sum(carry_ref[head, c], axis=0, keepdims=True)
                a = jnp.exp2(lb_ref[head, c] + between + carry)
                vh = jnp.where(_half_mask(vc.shape, 1, head % 2), vc, jnp.zeros_like(vc))
                part = lax.dot_general(vh, a.astype(BF16), TN, preferred_element_type=F32)
                pv = part if pv is None else pv + part
            acc_ref[p] += pv

    def pipelined(i, carries):
        c = qi - 1 - i
        attend(c + 1)
        return terms(c, carries, False)

    acc_ref[...] = jnp.zeros_like(acc_ref)
    carries = terms(qi, tuple(jnp.zeros((8, QB), F32) for _ in range(H_SB)), True)
    _fori_by_two(qi, pipelined, carries)
    attend(0)
    for p in range(H_SB // 2):
        o_ref[p * LANES:(p + 1) * LANES, :] = acc_ref[p].astype(BF16)


def _sb_attention(proj3):
    bsz, seq, _ = proj3.shape
    return pl.pallas_call(
        _sb_kernel,
        grid=(bsz, seq // QB),
        in_specs=[
            pl.BlockSpec((None, QB, PAIR_W), lambda b, i: (b, i, 0)),
            pl.BlockSpec((None, seq, PAIR_W), lambda b, i: (b, 0, 1)),
            pl.BlockSpec((None, seq, PAIR_W), lambda b, i: (b, 0, 2)),
        ],
        out_specs=pl.BlockSpec((None, PAIR_W, QB), lambda b, i: (b, 0, i)),
        out_shape=jax.ShapeDtypeStruct((bsz, PAIR_W, seq), BF16),
        scratch_shapes=[pltpu.VMEM((H_SB, QB, LANES), BF16),
                        pltpu.VMEM((H_SB, seq // KC, KC, QB), F32),
                        pltpu.VMEM((H_SB, seq // KC, KC, QB), BF16),
                        pltpu.VMEM((H_SB, seq // KC, 8, QB), F32),
                        pltpu.VMEM((H_SB // 2, LANES, QB), F32)],
        compiler_params=pltpu.CompilerParams(
            dimension_semantics=("parallel", "arbitrary"),
            vmem_limit_bytes=48 * 1024 * 1024),
        name="sb_attn",
    )(proj3, proj3, proj3)


def _fold8(x, op):
    return op(x.reshape(KC // 8, 8, QB), axis=0)


def _softmax_heads(qh_ref, k_ref, v_ref, s_ref, acc_ref, o_ref, qi, n_heads, table, shared_bias, shift):
    out_row = lax.broadcasted_iota(I32, (LANES, QB), 0)
    n_pairs = (n_heads + 1) // 2
    pair_heads = [[h for h in (2 * p, 2 * p + 1) if h < n_heads] for p in range(n_pairs)]

    def by_head_rows(vals):
        return vals[0] if len(vals) == 1 else jnp.where(out_row < HEAD_DIM, vals[0], vals[1])

    def score(c, maxes, diag):
        start = pl.multiple_of(c * KC, KC)
        shared = None if shared_bias is None else shared_bias(c)
        new = list(maxes)
        for p in range(n_pairs):
            kc = k_ref[pl.ds(start, KC), p * LANES:(p + 1) * LANES]
            for head in pair_heads[p]:
                s = lax.dot_general(kc, qh_ref[head], NT, preferred_element_type=F32) + table(head, diag)
                if shared is not None:
                    s = s + shared
                s_ref[head, c] = s
                new[head] = jnp.maximum(maxes[head], _fold8(s, jnp.max) + shift(c, head))
        return tuple(new)

    maxes = score(qi, tuple(jnp.full((8, QB), NEG_BIG, F32) for _ in range(n_heads)), True)
    maxes = _fori_by_two(qi, lambda c, mx: score(c, mx, False), maxes)
    m = [jnp.max(mx, axis=0, keepdims=True) for mx in maxes]

    acc_ref[...] = jnp.zeros_like(acc_ref)

    def attend(c, sums):
        start = pl.multiple_of(c * KC, KC)
        new = list(sums)
        for p in range(n_pairs):
            vc = v_ref[pl.ds(start, KC), p * LANES:(p + 1) * LANES]
            pv = None
            for head in pair_heads[p]:
                pr = jnp.exp2(s_ref[head, c] + (shift(c, head) - m[head]))
                new[head] = sums[head] + _fold8(pr, jnp.sum)
                vh = vc if len(pair_heads[p]) == 1 else jnp.where(
                    _half_mask(vc.shape, 1, head % 2), vc, jnp.zeros_like(vc))
                part = lax.dot_general(vh, pr.astype(BF16), TN, preferred_element_type=F32)
                pv = part if pv is None else pv + part
            acc_ref[p] += pv
        return tuple(new)

    sums = _fori_by_two(qi + 1, attend, tuple(jnp.zeros((8, QB), F32) for _ in range(n_heads)))
    for p in range(n_pairs):
        denom = by_head_rows([jnp.sum(sums[h], axis=0, keepdims=True) for h in pair_heads[p]])
        o_ref[p * LANES:(p + 1) * LANES, :] = (acc_ref[p] / denom).astype(BF16)


def _alibi_tables(table_ref, slopes, causal):
    row = lax.broadcasted_iota(I32, (KC, QB), 0)
    lane = lax.broadcasted_iota(I32, (KC, QB), 1)
    key_minus_query = (row - lane).astype(F32)
    for head, slope in enumerate(slopes):
        table_ref[head, 0] = slope * key_minus_query
        if causal:
            table_ref[head, 1] = jnp.where(row <= lane, slope * key_minus_query, NEG_BIG)


def _moba_kernel(q_ref, k_ref, v_ref, o_ref, kmean_ref, sel_ref, qh_ref, table_ref, s_ref, acc_ref):
    qi = pl.program_id(1)
    n_blocks = kmean_ref.shape[0]

    @pl.when(qi == 0)
    def _():
        for n in range(n_blocks):
            kb = k_ref[n * MOBA_BLOCK:(n + 1) * MOBA_BLOCK, :].astype(F32)
            kmean_ref[n:n + 1, :] = jnp.mean(kb, axis=0, keepdims=True)
        _alibi_tables(table_ref, SLOPES_MOBA, causal=True)

    blk = lax.broadcasted_iota(I32, (n_blocks, QB), 0)

    _masked_heads(q_ref, qh_ref, H_MOBA)

    for head in range(H_MOBA):
        cols = slice((head // 2) * LANES, (head // 2 + 1) * LANES)
        gate = lax.dot_general(kmean_ref[:, cols], qh_ref[head].astype(F32), NT,
                               preferred_element_type=F32, precision=lax.Precision.HIGHEST)
        rank = jnp.zeros((n_blocks, QB), I32)
        for mb in range(n_blocks):
            gm = gate[mb:mb + 1, :]
            beats = (gm > gate) | ((gm == gate) & (mb < blk))
            rank = rank + jnp.where(beats, (mb < qi).astype(I32), 0)
        selected = ((rank < MOBA_TOPK) & (blk < qi)) | (blk == qi)
        sel_ref[head] = jnp.where(selected, 0.0, NEG_BIG)

    def shift(c, head):
        return sel_ref[head, pl.ds(c, 1), :] + SLOPES_MOBA[head] * ((c - qi) * KC).astype(F32)

    _softmax_heads(qh_ref, k_ref, v_ref, s_ref, acc_ref, o_ref, qi, H_MOBA,
                   table=lambda head, diag: table_ref[head, 1 if diag else 0],
                   shared_bias=None, shift=shift)


def _moba_attention(proj3):
    bsz, seq, _ = proj3.shape
    n_blocks = seq // MOBA_BLOCK
    return pl.pallas_call(
        _moba_kernel,
        grid=(bsz, seq // QB),
        in_specs=[
            pl.BlockSpec((None, QB, PAIR_W), lambda b, i: (b, i, 3)),
            pl.BlockSpec((None, seq, PAIR_W), lambda b, i: (b, 0, 4)),
            pl.BlockSpec((None, seq, PAIR_W), lambda b, i: (b, 0, 5)),
        ],
        out_specs=pl.BlockSpec((None, PAIR_W, QB), lambda b, i: (b, 0, i)),
        out_shape=jax.ShapeDtypeStruct((bsz, PAIR_W, seq), BF16),
        scratch_shapes=[pltpu.VMEM((n_blocks, PAIR_W), F32),
                        pltpu.VMEM((H_MOBA, n_blocks, QB), F32),
                        pltpu.VMEM((H_MOBA, QB, LANES), BF16),
                        pltpu.VMEM((H_MOBA, 2, KC, QB), F32),
                        pltpu.VMEM((H_MOBA, seq // KC, KC, QB), F32),
                        pltpu.VMEM((PAIR_W // LANES, LANES, QB), F32)],
        compiler_params=pltpu.CompilerParams(
            dimension_semantics=("parallel", "arbitrary"),
            vmem_limit_bytes=48 * 1024 * 1024),
        name="moba_attn",
    )(proj3, proj3, proj3)


def _dsa_kernel(q_ref, k_ref, v_ref, qx_ref, kxq_ref, kx_ref, o_ref,
                kx2_ref, key_ref, hi_ref, lo_ref, mb_ref, tau_ref, cidx_ref, qh_ref, table_ref, s_ref, acc_ref, *, top):
    qi = pl.program_id(1)
    seq = k_ref.shape[0]

    @pl.when(qi == 0)
    def _():
        kx = kx_ref[...].astype(F32)
        dup = jnp.where(_half_mask(kx.shape, 1, 0), kx, pltpu.roll(kx, HEAD_DIM, 1))
        kx2_ref[...] = dup.astype(BF16)
        _alibi_tables(table_ref, SLOPES_DSA, causal=False)

    row = lax.broadcasted_iota(I32, (KC, QB), 0)
    lane = lax.broadcasted_iota(I32, (KC, QB), 1)
    causal_diag = row <= lane

    pick = (lax.broadcasted_iota(I32, (IDX_HEADS, LANES), 1)
            == lax.broadcasted_iota(I32, (IDX_HEADS, LANES), 0) + IDX_DIM).astype(BF16)
    w_t = lax.dot_general(pick, kxq_ref[...], NT, preferred_element_type=F32)

    def score_chunk(c, diag):
        start = pl.multiple_of(c * KC, KC)
        kc = kx2_ref[pl.ds(start, KC), :]
        sc = jnp.zeros((KC, QB), F32)
        for pp in range(IDX_HEADS // 2):
            qp = qx_ref[:, pp * LANES:(pp + 1) * LANES]
            for hh in range(2):
                h = 2 * pp + hh
                qh = jnp.where(_half_mask(qp.shape, 1, hh), qp, jnp.zeros_like(qp))
                lg = lax.dot_general(kc, qh, NT, preferred_element_type=F32)
                sc = sc + jnp.maximum(lg, 0.0) * w_t[h:h + 1, :]
        if diag:
            sc = jnp.where(causal_diag, sc, NEG_BIG)
        bits = pltpu.bitcast(sc, I32)
        key = jnp.where(bits < 0, bits ^ jnp.int32(0x7FFFFFFF), bits)
        key_ref[pl.ds(start, KC), :] = key
        hi_ref[pl.ds(start, KC), :] = lax.shift_right_arithmetic(key, 16).astype(I16)
        lo_ref[pl.ds(start, KC), :] = ((key & 0xFFFF) + I16_MIN).astype(I16)

    score_chunk(qi, True)

    def _score_body(c, carry):
        score_chunk(c, False)
        return carry

    lax.fori_loop(0, qi, _score_body, 0)

    shape16 = (KC // PACK16, PACK16, QB)

    def count(n, pred):
        acc8 = jnp.zeros((8, QB), I32)
        for c in range(n):
            hit = jnp.where(pred(key_ref[c * KC:(c + 1) * KC, :], c), 1, 0)
            acc8 = acc8 + jnp.sum(hit.reshape(KC // 8, 8, QB), axis=0)
        return jnp.sum(acc8, axis=0, keepdims=True)

    def packed_rows(x):
        return jnp.broadcast_to(x, (PACK16, QB)).astype(I16)[None]

    def count16(ref, n, pred):
        acc = jnp.zeros((PACK16, QB), I16)
        for c in range(n):
            hit = jnp.where(pred(ref[c * KC:(c + 1) * KC, :].reshape(shape16)), jnp.int16(1), jnp.int16(0))
            for r in range(KC // PACK16):
                acc = acc + hit[r]
        return jnp.sum(acc.astype(I32), axis=0, keepdims=True)

    def kth_largest16(ref, n, kth):
        def bit_step(i, tau):
            cand = tau + lax.shift_left(jnp.int32(1), 15 - i)
            cand16 = packed_rows(cand)
            cnt = count16(ref, n, lambda v: v >= cand16)
            return jnp.where(cnt >= kth, cand, tau)
        return lax.fori_loop(0, 16, bit_step, jnp.full((1, QB), I16_MIN, I32))

    def find_threshold(n):
        tau_hi = kth_largest16(hi_ref, n, top)
        tau_hi16 = packed_rows(tau_hi)
        above = count16(hi_ref, n, lambda v: v > tau_hi16)
        for c in range(n):
            rows = slice(c * KC, (c + 1) * KC)
            parked = jnp.where(hi_ref[rows, :].reshape(shape16) == tau_hi16,
                               lo_ref[rows, :].reshape(shape16), jnp.int16(I16_MIN))
            lo_ref[rows, :] = parked.reshape(KC, QB)
        tau_lo = kth_largest16(lo_ref, n, top - above)
        tau = lax.shift_left(tau_hi, 16) | (tau_lo - I16_MIN)
        tau_ref[...] = jnp.broadcast_to(tau, tau_ref.shape)

        cnt_ge = count(n, lambda keys, c: keys >= tau)
        cnt_gt = count(n, lambda keys, c: keys > tau)
        need = top - cnt_gt

        @pl.when(jnp.max(cnt_ge) > top)
        def _():
            def idx_step(i, x):
                cand = x + lax.shift_left(jnp.int32(1), 11 - i)
                cnt = count(n, lambda keys, c: (keys == tau) & ((c * KC + row) < cand))
                return jnp.where(cnt < need, cand, x)
            x = lax.fori_loop(0, 12, idx_step, jnp.zeros((1, QB), I32))
            cidx_ref[...] = jnp.broadcast_to(x, cidx_ref.shape)

    tau_ref[...] = jnp.full(tau_ref.shape, INT_MIN, I32)
    cidx_ref[...] = jnp.full(cidx_ref.shape, seq, I32)
    for n in range(2, seq // KC + 1):
        pl.when(qi == n - 1)(functools.partial(find_threshold, n))

    tau = tau_ref[0:1, :]
    cidx = cidx_ref[0:1, :]

    def mask_chunk(c, diag):
        start = pl.multiple_of(c * KC, KC)
        keys = key_ref[pl.ds(start, KC), :]
        keep = (keys > tau) | ((keys == tau) & ((c * KC + row) <= cidx))
        if diag:
            keep = keep & causal_diag
        mb_ref[pl.ds(start, KC), :] = jnp.where(keep, 0.0, NEG_BIG)

    mask_chunk(qi, True)

    def _mask_body(c, carry):
        mask_chunk(c, False)
        return carry

    lax.fori_loop(0, qi, _mask_body, 0)

    _masked_heads(q_ref, qh_ref, H_DSA)

    _softmax_heads(qh_ref, k_ref, v_ref, s_ref, acc_ref, o_ref, qi, H_DSA,
                   table=lambda head, diag: table_ref[head, 0],
                   shared_bias=lambda c: mb_ref[pl.ds(pl.multiple_of(c * KC, KC), KC), :],
                   shift=lambda c, head: SLOPES_DSA[head] * ((c - qi) * KC).astype(F32))


def _dsa_attention(proj3):
    bsz, seq, _ = proj3.shape
    top = min(DSA_TOPK_MAX, seq // 4)
    assert top == QB, "the first query block must keep every admissible key"
    return pl.pallas_call(
        functools.partial(_dsa_kernel, top=top),
        grid=(bsz, seq // QB),
        in_specs=[
            pl.BlockSpec((None, QB, PAIR_W), lambda b, i: (b, i, 6)),
            pl.BlockSpec((None, seq, PAIR_W), lambda b, i: (b, 0, 7)),
            pl.BlockSpec((None, seq, PAIR_W), lambda b, i: (b, 0, 8)),
            pl.BlockSpec((None, QB, IDX_HEADS * IDX_DIM), lambda b, i: (b, i, COL_QX // (IDX_HEADS * IDX_DIM))),
            pl.BlockSpec((None, QB, LANES), lambda b, i: (b, i, COL_KX // LANES)),
            pl.BlockSpec((None, seq, LANES), lambda b, i: (b, 0, COL_KX // LANES)),
        ],
        out_specs=pl.BlockSpec((None, PAIR_W, QB), lambda b, i: (b, 0, i)),
        out_shape=jax.ShapeDtypeStruct((bsz, PAIR_W, seq), BF16),
        scratch_shapes=[pltpu.VMEM((seq, LANES), BF16),
                        pltpu.VMEM((seq, QB), I32),
                        pltpu.VMEM((seq, QB), I16),
                        pltpu.VMEM((seq, QB), I16),
                        pltpu.VMEM((seq, QB), F32),
                        pltpu.VMEM((8, QB), I32),
                        pltpu.VMEM((8, QB), I32),
                        pltpu.VMEM((H_DSA, QB, LANES), BF16),
                        pltpu.VMEM((H_DSA, 1, KC, QB), F32),
                        pltpu.VMEM((H_DSA, seq // KC, KC, QB), F32),
                        pltpu.VMEM((PAIR_W // LANES, LANES, QB), F32)],
        compiler_params=pltpu.CompilerParams(
            dimension_semantics=("parallel", "arbitrary"),
            vmem_limit_bytes=48 * 1024 * 1024),
        name="dsa_attn",
    )(proj3, proj3, proj3, proj3, proj3, proj3)


def _merge_kernel(osb_ref, omb_ref, ods_ref, gsb_ref, gmb_ref, gds_ref, x_ref, mod_ref, gpost_ref,
                  wsb_ref, wmb_ref, wds_ref, wo_ref, out_ref):
    def branch(o_ref, w_ref, g_ref):
        y = lax.dot_general(o_ref[...], w_ref[...], TN, preferred_element_type=F32)
        return g_ref[...].astype(F32) * y

    merged = branch(osb_ref, wsb_ref, gsb_ref) + branch(omb_ref, wmb_ref, gmb_ref) + branch(ods_ref, wds_ref, gds_ref)
    y = jnp.dot(merged.astype(BF16), wo_ref[...], preferred_element_type=F32)
    out_ref[...] = x_ref[...] + mod_ref[2:3, :] * (_rms(y) * gpost_ref[...])


def _merge(o_sb, o_mb, o_ds, proj, x2d, mod, g_post, w_sb, w_mb, w_ds, w_o, layer, seq):
    t, d = x2d.shape
    tm = 1024
    per_b = seq // tm
    gate_blk = COL_GATE // d
    o_spec = pl.BlockSpec((None, PAIR_W, tm), lambda i: (i // per_b, 0, i % per_b))
    w_spec = pl.BlockSpec((None, PAIR_W, d), lambda i: (layer, 0, 0))
    return pl.pallas_call(
        _merge_kernel,
        grid=(t // tm,),
        in_specs=[
            o_spec, o_spec, o_spec,
            pl.BlockSpec((tm, d), lambda i: (i, gate_blk)),
            pl.BlockSpec((tm, d), lambda i: (i, gate_blk + 1)),
            pl.BlockSpec((tm, d), lambda i: (i, gate_blk + 2)),
            pl.BlockSpec((tm, d), lambda i: (i, 0)),
            pl.BlockSpec((None, None, 6, d), lambda i: (layer, i // per_b, 0, 0)),
            pl.BlockSpec((None, 1, d), lambda i: (layer, 0, 0)),
            w_spec, w_spec, w_spec,
            pl.BlockSpec((None, d, d), lambda i: (layer, 0, 0)),
        ],
        out_specs=pl.BlockSpec((tm, d), lambda i: (i, 0)),
        out_shape=jax.ShapeDtypeStruct((t, d), F32),
        compiler_params=pltpu.CompilerParams(
            dimension_semantics=("parallel",),
            vmem_limit_bytes=48 * 1024 * 1024),
        name="merge_out",
    )(o_sb, o_mb, o_ds, proj, proj, proj, x2d, mod, g_post, w_sb, w_mb, w_ds, w_o)


def _ffn_kernel(x_ref, mod_ref, gpre_ref, gpost_ref, wg_ref, wu_ref, wd_ref, out_ref, h_ref, acc_ref):
    xh = _rms(x_ref[...])
    h_ref[...] = ((xh * gpre_ref[...]) * (1.0 + mod_ref[4:5, :]) + mod_ref[3:4, :]).astype(BF16)
    for f in range(0, D_FF, FFN_TF):
        h = h_ref[...]
        gate = jnp.dot(h, wg_ref[:, f:f + FFN_TF], preferred_element_type=F32)
        up = jnp.dot(h, wu_ref[:, f:f + FFN_TF], preferred_element_type=F32)
        act = ((gate * jax.nn.sigmoid(gate)) * up).astype(BF16)
        part = jnp.dot(act, wd_ref[f:f + FFN_TF, :], preferred_element_type=F32)
        if f == 0:
            acc_ref[...] = part
        else:
            acc_ref[...] += part
    y = acc_ref[...]
    out_ref[...] = x_ref[...] + mod_ref[5:6, :] * (_rms(y) * gpost_ref[...])


def _ffn(x2d, mod, g_pre, g_post, w_up, w_down, layer, seq):
    t, d = x2d.shape
    tm = 512
    per_b = seq // tm
    return pl.pallas_call(
        _ffn_kernel,
        grid=(t // tm,),
        in_specs=[
            pl.BlockSpec((tm, d), lambda i: (i, 0)),
            pl.BlockSpec((None, None, 6, d), lambda i: (layer, i // per_b, 0, 0)),
            pl.BlockSpec((None, 1, d), lambda i: (layer, 0, 0)),
            pl.BlockSpec((None, 1, d), lambda i: (layer, 0, 0)),
            pl.BlockSpec((None, d, D_FF), lambda i: (layer, 0, 0)),
            pl.BlockSpec((None, d, D_FF), lambda i: (layer, 0, 1)),
            pl.BlockSpec((None, D_FF, d), lambda i: (layer, 0, 0)),
        ],
        out_specs=pl.BlockSpec((tm, d), lambda i: (i, 0)),
        out_shape=jax.ShapeDtypeStruct((t, d), F32),
        scratch_shapes=[pltpu.VMEM((tm, d), BF16), pltpu.VMEM((tm, d), F32)],
        compiler_params=pltpu.CompilerParams(
            dimension_semantics=("parallel",),
            vmem_limit_bytes=56 * 1024 * 1024),
        name="ffn",
    )(x2d, mod, g_pre, g_post, w_up, w_up, w_down)


def _pad_last(w, n):
    return jnp.pad(w, [(0, 0)] * (w.ndim - 1) + [(0, n - w.shape[-1])])


def _pad_rows(w, n):
    return jnp.pad(w, ((0, 0), (0, n - w.shape[1]), (0, 0)))


def _layout_w_in(w_in, b_gate):
    sizes = [W_SB] * 3 + [W_MOBA] * 3 + [W_DSA] * 3 + [IDX_HEADS * IDX_DIM, IDX_DIM, IDX_HEADS, N_BRANCH * D_MODEL]
    points = [int(v) for v in np.cumsum(sizes)[:-1]]
    (q_sb, k_sb, v_sb, q_mb, k_mb, v_mb, q_ds, k_ds, v_ds, q_ix, k_ix, w_ix, gates) = jnp.split(w_in, points, axis=-1)
    scale = HEAD_DIM ** -0.5 * LOG2E
    cols = [q_sb * scale, k_sb, v_sb,
            _pad_last(q_mb * scale, PAIR_W), _pad_last(k_mb, PAIR_W), _pad_last(v_mb, PAIR_W),
            _pad_last(q_ds * scale, PAIR_W), _pad_last(k_ds, PAIR_W), _pad_last(v_ds, PAIR_W),
            _pad_last(jnp.concatenate([k_ix, w_ix], axis=-1), LANES), q_ix, gates]
    w_all = jnp.concatenate(cols, axis=-1).astype(BF16)
    bias = jnp.concatenate([jnp.zeros((b_gate.shape[0], COL_GATE), F32), b_gate], axis=-1)[:, None, :]
    return w_all, bias


def kernel(x, c, w_ada, b_ada, g_pre_mix, g_post_mix, w_in, b_gate, w_proj_sb, w_proj_moba,
           w_proj_dsa, w_o, g_pre_ffn, g_post_ffn, w_up, w_down):
    bsz, seq, d = x.shape
    depth = w_ada.shape[0]
    assert d == D_MODEL and seq % QB == 0 and QB == MOBA_BLOCK and KC == QB
    mod = _ada(c, w_ada, b_ada).reshape(depth, bsz, 6, d)
    w_all, bias = _layout_w_in(w_in, b_gate)
    assert w_all.shape[-1] == N_PROJ
    w_sb = w_proj_sb.astype(BF16)
    w_mb = _pad_rows(w_proj_moba, PAIR_W).astype(BF16)
    w_ds = _pad_rows(w_proj_dsa, PAIR_W).astype(BF16)
    w_o, w_up, w_down = w_o.astype(BF16), w_up.astype(BF16), w_down.astype(BF16)
    rows = lambda g: g[:, None, :]
    x2d = x.reshape(bsz * seq, d)
    for layer in range(depth):
        proj = _inproj(x2d, mod, rows(g_pre_mix), w_all, bias, layer, seq)
        proj3 = proj.reshape(bsz, seq, N_PROJ)
        o_sb = _sb_attention(proj3)
        o_mb = _moba_attention(proj3)
        o_ds = _dsa_attention(proj3)
        x2d = _merge(o_sb, o_mb, o_ds, proj, x2d, mod, rows(g_post_mix), w_sb, w_mb, w_ds, w_o, layer, seq)
        x2d = _ffn(x2d, mod, rows(g_pre_ffn), rows(g_post_ffn), w_up, w_down, layer, seq)
    return x2d.reshape(bsz, seq, d)
```

```python
import functools

import numpy as np
import jax
import jax.numpy as jnp
from jax import lax
from jax.experimental import pallas as pl
from jax.experimental.pallas import tpu as pltpu

F32 = jnp.float32
BF16 = jnp.bfloat16
I32 = jnp.int32
I16 = jnp.int16

D_MODEL = 1024
HEAD_DIM = 64
H_SB, H_MOBA, H_DSA = 6, 5, 5
W_SB, W_MOBA, W_DSA = H_SB * HEAD_DIM, H_MOBA * HEAD_DIM, H_DSA * HEAD_DIM
MOBA_BLOCK = 256
MOBA_TOPK = 3
DSA_TOPK_MAX = 256
IDX_HEADS = 8
IDX_DIM = 64
D_FF = 2816
N_BRANCH = 3
RMS_EPS = 1e-6
NEG_BIG = -1e30
ALIBI_HEADS = H_MOBA + H_DSA

LANES = 128
PAIR_W = 3 * LANES
QB = 256
KC = 256
INT_MIN = -(2 ** 31)
I16_MIN = -(2 ** 15)
PACK16 = 16

COL_KX = 9 * PAIR_W
COL_QX = COL_KX + LANES
COL_GATE = COL_QX + IDX_HEADS * IDX_DIM
N_PROJ = COL_GATE + N_BRANCH * D_MODEL
PROJ_TN = 1024
FFN_TF = 256

NT = (((1,), (1,)), ((), ()))
TN = (((0,), (0,)), ((), ()))

LOG2E = 1.4426950408889634
_ALIBI = [float(2.0 ** (-8.0 * h / ALIBI_HEADS)) * LOG2E for h in range(1, ALIBI_HEADS + 1)]
SLOPES_MOBA = _ALIBI[0::2]
SLOPES_DSA = _ALIBI[1::2]


def _rms(x):
    return x * lax.rsqrt(jnp.mean(x * x, axis=-1, keepdims=True) + RMS_EPS)


def _fori_by_two(n, body, init):
    def two(i, state):
        return body(2 * i + 1, body(2 * i, state))
    state = lax.fori_loop(0, n // 2, two, init)
    return lax.fori_loop(2 * (n // 2), n, body, state)


def _half_mask(shape, lane_axis, hh):
    lane = lax.broadcasted_iota(I32, shape, lane_axis)
    return (lane < HEAD_DIM) if hh == 0 else (lane >= HEAD_DIM)


def _ada_kernel(c_ref, w_ref, b_ref, o_ref):
    c = c_ref[...]
    ca = c * jax.nn.sigmoid(c)
    o_ref[...] = jnp.dot(ca, w_ref[...], preferred_element_type=F32) + b_ref[...]


def _ada(c, w_ada, b_ada):
    depth, d, n = w_ada.shape
    bsz = c.shape[0]
    tn = 512
    return pl.pallas_call(
        _ada_kernel,
        grid=(depth, n // tn),
        in_specs=[
            pl.BlockSpec((bsz, d), lambda l, j: (0, 0)),
            pl.BlockSpec((None, d, tn), lambda l, j: (l, 0, j)),
            pl.BlockSpec((None, 1, tn), lambda l, j: (l, 0, j)),
        ],
        out_specs=pl.BlockSpec((None, bsz, tn), lambda l, j: (l, 0, j)),
        out_shape=jax.ShapeDtypeStruct((depth, bsz, n), F32),
        name="ada_mod",
    )(c, w_ada, b_ada.reshape(depth, 1, n))


def _inproj_kernel(x_ref, mod_ref, g_ref, w_ref, b_ref, o_ref):
    xh = _rms(x_ref[...])
    h = ((xh * g_ref[...]) * (1.0 + mod_ref[1:2, :]) + mod_ref[0:1, :]).astype(BF16)
    for n0 in range(0, N_PROJ, PROJ_TN):
        cols = slice(n0, n0 + PROJ_TN)
        acc = jnp.dot(h, w_ref[:, cols], preferred_element_type=F32) + b_ref[:, cols]
        if n0 >= COL_GATE:
            acc = jax.nn.sigmoid(acc)
        o_ref[:, cols] = acc.astype(BF16)


def _inproj(x2d, mod, g_pre, w_all, bias, layer, seq):
    t, d = x2d.shape
    tm = 512
    n = w_all.shape[-1]
    per_b = seq // tm
    return pl.pallas_call(
        _inproj_kernel,
        grid=(t // tm,),
        in_specs=[
            pl.BlockSpec((tm, d), lambda i: (i, 0)),
            pl.BlockSpec((None, None, 6, d), lambda i: (layer, i // per_b, 0, 0)),
            pl.BlockSpec((None, 1, d), lambda i: (layer, 0, 0)),
            pl.BlockSpec((None, d, n), lambda i: (layer, 0, 0)),
            pl.BlockSpec((None, 1, n), lambda i: (layer, 0, 0)),
        ],
        out_specs=pl.BlockSpec((tm, n), lambda i: (i, 0)),
        out_shape=jax.ShapeDtypeStruct((t, n), BF16),
        compiler_params=pltpu.CompilerParams(
            dimension_semantics=("parallel",),
            vmem_limit_bytes=56 * 1024 * 1024),
        name="in_proj",
    )(x2d, mod, g_pre, w_all, bias)


def _masked_heads(ref, dst_ref, n_heads):
    for head in range(n_heads):
        grp = ref[:, (head // 2) * LANES:(head // 2 + 1) * LANES]
        dst_ref[head] = jnp.where(_half_mask(grp.shape, 1, head % 2), grp, jnp.zeros_like(grp))


def _sb_kernel(q_ref, k_ref, v_ref, o_ref, qh_ref, lb_ref, l1m_ref, carry_ref, acc_ref):
    qi = pl.program_id(1)
    row = lax.broadcasted_iota(I32, (KC, QB), 0)
    lane = lax.broadcasted_iota(I32, (KC, QB), 1)
    past_diag = row < lane
    upper = (lax.broadcasted_iota(I32, (KC, KC), 1) > lax.broadcasted_iota(I32, (KC, KC), 0)).astype(BF16)

    _masked_heads(q_ref, qh_ref, H_SB)

    def terms(c, carries, diag):
        start = pl.multiple_of(c * KC, KC)
        new = list(carries)
        for p in range(H_SB // 2):
            kc = k_ref[pl.ds(start, KC), p * LANES:(p + 1) * LANES]
            for head in (2 * p, 2 * p + 1):
                z = lax.dot_general(kc, qh_ref[head], NT, preferred_element_type=F32)
                neg_abs_z = pltpu.bitcast(pltpu.bitcast(z, I32) | jnp.int32(INT_MIN), F32)
                log_beta = jnp.minimum(z, 0.0) - jnp.log2(1.0 + jnp.exp2(neg_abs_z))
                log_1m = log_beta - z
                if diag:
                    log_1m = jnp.where(past_diag, log_1m, 0.0)
                    log_beta = jnp.where(past_diag, log_beta, NEG_BIG)
                lb_ref[head, c] = log_beta
                l1m_ref[head, c] = log_1m.astype(BF16)
                carry_ref[head, c] = carries[head]
                new[head] = carries[head] + _fold8(log_1m, jnp.sum)
        return tuple(new)

    def attend(c):
        start = pl.multiple_of(c * KC, KC)
        for p in range(H_SB // 2):
            vc = v_ref[pl.ds(start, KC), p * LANES:(p + 1) * LANES]
            pv = None
            for head in (2 * p, 2 * p + 1):
                between = jnp.dot(upper, l1m_ref[head, c], preferred_element_type=F32)
                carry = jnp.sum(carry_ref[head, c], axis=0, keepdims=True)
                a = jnp.exp2(lb_ref[head, c] + between + carry)
                vh = jnp.where(_half_mask(vc.shape, 1, head % 2), vc, jnp.zeros_like(vc))
                part = lax.dot_general(vh, a.astype(BF16), TN, preferred_element_type=F32)
                pv = part if pv is None else pv + part
            acc_ref[p] += pv

    def pipelined(i, carries):
        c = qi - 1 - i
        attend(c + 1)
        return terms(c, carries, False)

    acc_ref[...] = jnp.zeros_like(acc_ref)
    carries = terms(qi, tuple(jnp.zeros((8, QB), F32) for _ in range(H_SB)), True)
    _fori_by_two(qi, pipelined, carries)
    attend(0)
    for p in range(H_SB // 2):
        o_ref[p * LANES:(p + 1) * LANES, :] = acc_ref[p].astype(BF16)


def _sb_attention(proj3):
    bsz, seq, _ = proj3.shape
    return pl.pallas_call(
        _sb_kernel,
        grid=(bsz, seq // QB),
        in_specs=[
            pl.BlockSpec((None, QB, PAIR_W), lambda b, i: (b, i, 0)),
            pl.BlockSpec((None, seq, PAIR_W), lambda b, i: (b, 0, 1)),
            pl.BlockSpec((None, seq, PAIR_W), lambda b, i: (b, 0, 2)),
        ],
        out_specs=pl.BlockSpec((None, PAIR_W, QB), lambda b, i: (b, 0, i)),
        out_shape=jax.ShapeDtypeStruct((bsz, PAIR_W, seq), BF16),
        scratch_shapes=[pltpu.VMEM((H_SB, QB, LANES), BF16),
                        pltpu.VMEM((H_SB, seq // KC, KC, QB), F32),
                        pltpu.VMEM((H_SB, seq // KC, KC, QB), BF16),
                        pltpu.VMEM((H_SB, seq // KC, 8, QB), F32),
                        pltpu.VMEM((H_SB // 2, LANES, QB), F32)],
        compiler_params=pltpu.CompilerParams(
            dimension_semantics=("parallel", "arbitrary"),
            vmem_limit_bytes=48 * 1024 * 1024),
        name="sb_attn",
    )(proj3, proj3, proj3)


def _fold8(x, op):
    return op(x.reshape(KC // 8, 8, QB), axis=0)


def _softmax_heads(qh_ref, k_ref, v_ref, s_ref, acc_ref, o_ref, qi, n_heads, table, shared_bias, shift):
    out_row = lax.broadcasted_iota(I32, (LANES, QB), 0)
    n_pairs = (n_heads + 1) // 2
    pair_heads = [[h for h in (2 * p, 2 * p + 1) if h < n_heads] for p in range(n_pairs)]

    def by_head_rows(vals):
        return vals[0] if len(vals) == 1 else jnp.where(out_row < HEAD_DIM, vals[0], vals[1])

    def score(c, maxes, diag):
        start = pl.multiple_of(c * KC, KC)
        shared = None if shared_bias is None else shared_bias(c)
        new = list(maxes)
        for p in range(n_pairs):
            kc = k_ref[pl.ds(start, KC), p * LANES:(p + 1) * LANES]
            for head in pair_heads[p]:
                s = lax.dot_general(kc, qh_ref[head], NT, preferred_element_type=F32) + table(head, diag)
                if shared is not None:
                    s = s + shared
                s_ref[head, c] = s
                new[head] = jnp.maximum(maxes[head], _fold8(s, jnp.max) + shift(c, head))
        return tuple(new)

    maxes = score(qi, tuple(jnp.full((8, QB), NEG_BIG, F32) for _ in range(n_heads)), True)
    maxes = _fori_by_two(qi, lambda c, mx: score(c, mx, False), maxes)
    m = [jnp.max(mx, axis=0, keepdims=True) for mx in maxes]

    acc_ref[...] = jnp.zeros_like(acc_ref)

    def attend(c, sums):
        start = pl.multiple_of(c * KC, KC)
        new = list(sums)
        for p in range(n_pairs):
            vc = v_ref[pl.ds(start, KC), p * LANES:(p + 1) * LANES]
            pv = None
            for head in pair_heads[p]:
                pr = jnp.exp2(s_ref[head, c] + (shift(c, head) - m[head]))
                new[head] = sums[head] + _fold8(pr, jnp.sum)
                vh = vc if len(pair_heads[p]) == 1 else jnp.where(
                    _half_mask(vc.shape, 1, head % 2), vc, jnp.zeros_like(vc))
                part = lax.dot_general(vh, pr.astype(BF16), TN, preferred_element_type=F32)
                pv = part if pv is None else pv + part
            acc_ref[p] += pv
        return tuple(new)

    sums = _fori_by_two(qi + 1, attend, tuple(jnp.zeros((8, QB), F32) for _ in range(n_heads)))
    for p in range(n_pairs):
        denom = by_head_rows([jnp.sum(sums[h], axis=0, keepdims=True) for h in pair_heads[p]])
        o_ref[p * LANES:(p + 1) * LANES, :] = (acc_ref[p] / denom).astype(BF16)


def _alibi_tables(table_ref, slopes, causal):
    row = lax.broadcasted_iota(I32, (KC, QB), 0)
    lane = lax.broadcasted_iota(I32, (KC, QB), 1)
    key_minus_query = (row - lane).astype(F32)
    for head, slope in enumerate(slopes):
        table_ref[head, 0] = slope * key_minus_query
        if causal:
            table_ref[head, 1] = jnp.where(row <= lane, slope * key_minus_query, NEG_BIG)


def _moba_kernel(q_ref, k_ref, v_ref, o_ref, kmean_ref, sel_ref, qh_ref, table_ref, s_ref, acc_ref):
    qi = pl.program_id(1)
    n_blocks = kmean_ref.shape[0]

    @pl.when(qi == 0)
    def _():
        for n in range(n_blocks):
            kb = k_ref[n * MOBA_BLOCK:(n + 1) * MOBA_BLOCK, :].astype(F32)
            kmean_ref[n:n + 1, :] = jnp.mean(kb, axis=0, keepdims=True)
        _alibi_tables(table_ref, SLOPES_MOBA, causal=True)

    blk = lax.broadcasted_iota(I32, (n_blocks, QB), 0)

    _masked_heads(q_ref, qh_ref, H_MOBA)

    for head in range(H_MOBA):
        cols = slice((head // 2) * LANES, (head // 2 + 1) * LANES)
        gate = lax.dot_general(kmean_ref[:, cols], qh_ref[head].astype(F32), NT,
                               preferred_element_type=F32, precision=lax.Precision.HIGHEST)
        rank = jnp.zeros((n_blocks, QB), I32)
        for mb in range(n_blocks):
            gm = gate[mb:mb + 1, :]
            beats = (gm > gate) | ((gm == gate) & (mb < blk))
            rank = rank + jnp.where(beats, (mb < qi).astype(I32), 0)
        selected = ((rank < MOBA_TOPK) & (blk < qi)) | (blk == qi)
        sel_ref[head] = jnp.where(selected, 0.0, NEG_BIG)

    def shift(c, head):
        return sel_ref[head, pl.ds(c, 1), :] + SLOPES_MOBA[head] * ((c - qi) * KC).astype(F32)

    _softmax_heads(qh_ref, k_ref, v_ref, s_ref, acc_ref, o_ref, qi, H_MOBA,
                   table=lambda head, diag: table_ref[head, 1 if diag else 0],
                   shared_bias=None, shift=shift)


def _moba_attention(proj3):
    bsz, seq, _ = proj3.shape
    n_blocks = seq // MOBA_BLOCK
    return pl.pallas_call(
        _moba_kernel,
        grid=(bsz, seq // QB),
        in_specs=[
            pl.BlockSpec((None, QB, PAIR_W), lambda b, i: (b, i, 3)),
            pl.BlockSpec((None, seq, PAIR_W), lambda b, i: (b, 0, 4)),
            pl.BlockSpec((None, seq, PAIR_W), lambda b, i: (b, 0, 5)),
        ],
        out_specs=pl.BlockSpec((None, PAIR_W, QB), lambda b, i: (b, 0, i)),
        out_shape=jax.ShapeDtypeStruct((bsz, PAIR_W, seq), BF16),
        scratch_shapes=[pltpu.VMEM((n_blocks, PAIR_W), F32),
                        pltpu.VMEM((H_MOBA, n_blocks, QB), F32),
                        pltpu.VMEM((H_MOBA, QB, LANES), BF16),
                        pltpu.VMEM((H_MOBA, 2, KC, QB), F32),
                        pltpu.VMEM((H_MOBA, seq // KC, KC, QB), F32),
                        pltpu.VMEM((PAIR_W // LANES, LANES, QB), F32)],
        compiler_params=pltpu.CompilerParams(
            dimension_semantics=("parallel", "arbitrary"),
            vmem_limit_bytes=48 * 1024 * 1024),
        name="moba_attn",
    )(proj3, proj3, proj3)


def _dsa_kernel(q_ref, k_ref, v_ref, qx_ref, kxq_ref, kx_ref, o_ref,
                kx2_ref, key_ref, hi_ref, lo_ref, mb_ref, tau_ref, cidx_ref, qh_ref, table_ref, s_ref, acc_ref, *, top):
    qi = pl.program_id(1)
    seq = k_ref.shape[0]

    @pl.when(qi == 0)
    def _():
        kx = kx_ref[...].astype(F32)
        dup = jnp.where(_half_mask(kx.shape, 1, 0), kx, pltpu.roll(kx, HEAD_DIM, 1))
        kx2_ref[...] = dup.astype(BF16)
        _alibi_tables(table_ref, SLOPES_DSA, causal=False)

    row = lax.broadcasted_iota(I32, (KC, QB), 0)
    lane = lax.broadcasted_iota(I32, (KC, QB), 1)
    causal_diag = row <= lane

    pick = (lax.broadcasted_iota(I32, (IDX_HEADS, LANES), 1)
            == lax.broadcasted_iota(I32, (IDX_HEADS, LANES), 0) + IDX_DIM).astype(BF16)
    w_t = lax.dot_general(pick, kxq_ref[...], NT, preferred_element_type=F32)

    def score_rows(c, size, diag):
        start = pl.multiple_of(c * KC, KC)
        kc = kx2_ref[pl.ds(start, size), :]
        sc = jnp.zeros((size, QB), F32)
        for pp in range(IDX_HEADS // 2):
            qp = qx_ref[:, pp * LANES:(pp + 1) * LANES]
            for hh in range(2):
                h = 2 * pp + hh
                qh = jnp.where(_half_mask(qp.shape, 1, hh), qp, jnp.zeros_like(qp))
                lg = lax.dot_general(kc, qh, NT, preferred_element_type=F32)
                sc = sc + jnp.maximum(lg, 0.0) * w_t[h:h + 1, :]
        if diag:
            sc = jnp.where(causal_diag, sc, NEG_BIG)
        bits = pltpu.bitcast(sc, I32)
        key = jnp.where(bits < 0, bits ^ jnp.int32(0x7FFFFFFF), bits)
        key_ref[pl.ds(start, size), :] = key
        hi_ref[pl.ds(start, size), :] = lax.shift_right_arithmetic(key, 16).astype(I16)
        lo_ref[pl.ds(start, size), :] = ((key & 0xFFFF) + I16_MIN).astype(I16)

    score_rows(qi, KC, True)

    def _score_two(i, carry):
        score_rows(2 * i, 2 * KC, False)
        return carry

    def _score_one(c, carry):
        score_rows(c, KC, False)
        return carry

    lax.fori_loop(0, qi // 2, _score_two, 0)
    lax.fori_loop(2 * (qi // 2), qi, _score_one, 0)

    shape16 = (KC // PACK16, PACK16, QB)

    def count(n, pred):
        acc8 = jnp.zeros((8, QB), I32)
        for c in range(n):
            hit = jnp.where(pred(key_ref[c * KC:(c + 1) * KC, :], c), 1, 0)
            acc8 = acc8 + jnp.sum(hit.reshape(KC // 8, 8, QB), axis=0)
        return jnp.sum(acc8, axis=0, keepdims=True)

    def packed_rows(x):
        return jnp.broadcast_to(x, (PACK16, QB)).astype(I16)[None]

    def count16(ref, n, pred):
        accs = [jnp.zeros((PACK16, QB), I16) for _ in range(4)]
        for c in range(n):
            hit = jnp.where(pred(ref[c * KC:(c + 1) * KC, :].reshape(shape16)), jnp.int16(1), jnp.int16(0))
            for r in range(KC // PACK16):
                accs[r % 4] = accs[r % 4] + hit[r]
        acc = (accs[0] + accs[1]) + (accs[2] + accs[3])
        return jnp.sum(acc.astype(I32), axis=0, keepdims=True)

    def kth_largest16(ref, n, kth):
        def bit_step(i, tau):
            cand = tau + lax.shift_left(jnp.int32(1), 15 - i)
            cand16 = packed_rows(cand)
            cnt = count16(ref, n, lambda v: v >= cand16)
            return jnp.where(cnt >= kth, cand, tau)
        return lax.fori_loop(0, 16, bit_step, jnp.full((1, QB), I16_MIN, I32))

    def find_threshold(n):
        tau_hi = kth_largest16(hi_ref, n, top)
        tau_hi16 = packed_rows(tau_hi)
        above = count16(hi_ref, n, lambda v: v > tau_hi16)
        for c in range(n):
            rows = slice(c * KC, (c + 1) * KC)
            parked = jnp.where(hi_ref[rows, :].reshape(shape16) == tau_hi16,
                               lo_ref[rows, :].reshape(shape16), jnp.int16(I16_MIN))
            lo_ref[rows, :] = parked.reshape(KC, QB)
        tau_lo = kth_largest16(lo_ref, n, top - above)
        tau = lax.shift_left(tau_hi, 16) | (tau_lo - I16_MIN)
        tau_ref[...] = jnp.broadcast_to(tau, tau_ref.shape)

        cnt_ge = count(n, lambda keys, c: keys >= tau)
        cnt_gt = count(n, lambda keys, c: keys > tau)
        need = top - cnt_gt

        @pl.when(jnp.max(cnt_ge) > top)
        def _():
            def idx_step(i, x):
                cand = x + lax.shift_left(jnp.int32(1), 11 - i)
                cnt = count(n, lambda keys, c: (keys == tau) & ((c * KC + row) < cand))
                return jnp.where(cnt < need, cand, x)
            x = lax.fori_loop(0, 12, idx_step, jnp.zeros((1, QB), I32))
            cidx_ref[...] = jnp.broadcast_to(x, cidx_ref.shape)

    tau_ref[...] = jnp.full(tau_ref.shape, INT_MIN, I32)
    cidx_ref[...] = jnp.full(cidx_ref.shape, seq, I32)
    for n in range(2, seq // KC + 1):
        pl.when(qi == n - 1)(functools.partial(find_threshold, n))

    tau = tau_ref[0:1, :]
    cidx = cidx_ref[0:1, :]

    def mask_chunk(c, diag):
        start = pl.multiple_of(c * KC, KC)
        keys = key_ref[pl.ds(start, KC), :]
        keep = (keys > tau) | ((keys == tau) & ((c * KC + row) <= cidx))
        if diag:
            keep = keep & causal_diag
        mb_ref[pl.ds(start, KC), :] = jnp.where(keep, 0.0, NEG_BIG)

    mask_chunk(qi, True)

    def _mask_body(c, carry):
        mask_chunk(c, False)
        return carry

    lax.fori_loop(0, qi, _mask_body, 0)

    _masked_heads(q_ref, qh_ref, H_DSA)

    _softmax_heads(qh_ref, k_ref, v_ref, s_ref, acc_ref, o_ref, qi, H_DSA,
                   table=lambda head, diag: table_ref[head, 0],
                   shared_bias=lambda c: mb_ref[pl.ds(pl.multiple_of(c * KC, KC), KC), :],
                   shift=lambda c, head: SLOPES_DSA[head] * ((c - qi) * KC).astype(F32))


def _dsa_attention(proj3):
    bsz, seq, _ = proj3.shape
    top = min(DSA_TOPK_MAX, seq // 4)
    assert top == QB, "the first query block must keep every admissible key"
    return pl.pallas_call(
        functools.partial(_dsa_kernel, top=top),
        grid=(bsz, seq // QB),
        in_specs=[
            pl.BlockSpec((None, QB, PAIR_W), lambda b, i: (b, i, 6)),
            pl.BlockSpec((None, seq, PAIR_W), lambda b, i: (b, 0, 7)),
            pl.BlockSpec((None, seq, PAIR_W), lambda b, i: (b, 0, 8)),
            pl.BlockSpec((None, QB, IDX_HEADS * IDX_DIM), lambda b, i: (b, i, COL_QX // (IDX_HEADS * IDX_DIM))),
            pl.BlockSpec((None, QB, LANES), lambda b, i: (b, i, COL_KX // LANES)),
            pl.BlockSpec((None, seq, LANES), lambda b, i: (b, 0, COL_KX // LANES)),
        ],
        out_specs=pl.BlockSpec((None, PAIR_W, QB), lambda b, i: (b, 0, i)),
        out_shape=jax.ShapeDtypeStruct((bsz, PAIR_W, seq), BF16),
        scratch_shapes=[pltpu.VMEM((seq, LANES), BF16),
                        pltpu.VMEM((seq, QB), I32),
                        pltpu.VMEM((seq, QB), I16),
                        pltpu.VMEM((seq, QB), I16),
                        pltpu.VMEM((seq, QB), F32),
                        pltpu.VMEM((8, QB), I32),
                        pltpu.VMEM((8, QB), I32),
                        pltpu.VMEM((H_DSA, QB, LANES), BF16),
                        pltpu.VMEM((H_DSA, 1, KC, QB), F32),
                        pltpu.VMEM((H_DSA, seq // KC, KC, QB), F32),
                        pltpu.VMEM((PAIR_W // LANES, LANES, QB), F32)],
        compiler_params=pltpu.CompilerParams(
            dimension_semantics=("parallel", "arbitrary"),
            vmem_limit_bytes=48 * 1024 * 1024),
        name="dsa_attn",
    )(proj3, proj3, proj3, proj3, proj3, proj3)


def _merge_kernel(osb_ref, omb_ref, ods_ref, gsb_ref, gmb_ref, gds_ref, x_ref, mod_ref, gpost_ref,
                  wsb_ref, wmb_ref, wds_ref, wo_ref, out_ref):
    def branch(o_ref, w_ref, g_ref):
        y = lax.dot_general(o_ref[...], w_ref[...], TN, preferred_element_type=F32)
        return g_ref[...].astype(F32) * y

    merged = branch(osb_ref, wsb_ref, gsb_ref) + branch(omb_ref, wmb_ref, gmb_ref) + branch(ods_ref, wds_ref, gds_ref)
    y = jnp.dot(merged.astype(BF16), wo_ref[...], preferred_element_type=F32)
    out_ref[...] = x_ref[...] + mod_ref[2:3, :] * (_rms(y) * gpost_ref[...])


def _merge(o_sb, o_mb, o_ds, proj, x2d, mod, g_post, w_sb, w_mb, w_ds, w_o, layer, seq):
    t, d = x2d.shape
    tm = 1024
    per_b = seq // tm
    gate_blk = COL_GATE // d
    o_spec = pl.BlockSpec((None, PAIR_W, tm), lambda i: (i // per_b, 0, i % per_b))
    w_spec = pl.BlockSpec((None, PAIR_W, d), lambda i: (layer, 0, 0))
    return pl.pallas_call(
        _merge_kernel,
        grid=(t // tm,),
        in_specs=[
            o_spec, o_spec, o_spec,
            pl.BlockSpec((tm, d), lambda i: (i, gate_blk)),
            pl.BlockSpec((tm, d), lambda i: (i, gate_blk + 1)),
            pl.BlockSpec((tm, d), lambda i: (i, gate_blk + 2)),
            pl.BlockSpec((tm, d), lambda i: (i, 0)),
            pl.BlockSpec((None, None, 6, d), lambda i: (layer, i // per_b, 0, 0)),
            pl.BlockSpec((None, 1, d), lambda i: (layer, 0, 0)),
            w_spec, w_spec, w_spec,
            pl.BlockSpec((None, d, d), lambda i: (layer, 0, 0)),
        ],
        out_specs=pl.BlockSpec((tm, d), lambda i: (i, 0)),
        out_shape=jax.ShapeDtypeStruct((t, d), F32),
        compiler_params=pltpu.CompilerParams(
            dimension_semantics=("parallel",),
            vmem_limit_bytes=48 * 1024 * 1024),
        name="merge_out",
    )(o_sb, o_mb, o_ds, proj, proj, proj, x2d, mod, g_post, w_sb, w_mb, w_ds, w_o)


def _ffn_kernel(x_ref, mod_ref, gpre_ref, gpost_ref, wg_ref, wu_ref, wd_ref, out_ref, h_ref, acc_ref):
    xh = _rms(x_ref[...])
    h_ref[...] = ((xh * gpre_ref[...]) * (1.0 + mod_ref[4:5, :]) + mod_ref[3:4, :]).astype(BF16)
    for f in range(0, D_FF, FFN_TF):
        h = h_ref[...]
        gate = jnp.dot(h, wg_ref[:, f:f + FFN_TF], preferred_element_type=F32)
        up = jnp.dot(h, wu_ref[:, f:f + FFN_TF], preferred_element_type=F32)
        act = ((gate * jax.nn.sigmoid(gate)) * up).astype(BF16)
        part = jnp.dot(act, wd_ref[f:f + FFN_TF, :], preferred_element_type=F32)
        if f == 0:
            acc_ref[...] = part
        else:
            acc_ref[...] += part
    y = acc_ref[...]
    out_ref[...] = x_ref[...] + mod_ref[5:6, :] * (_rms(y) * gpost_ref[...])


def _ffn(x2d, mod, g_pre, g_post, w_up, w_down, layer, seq):
    t, d = x2d.shape
    tm = 512
    per_b = seq // tm
    return pl.pallas_call(
        _ffn_kernel,
        grid=(t // tm,),
        in_specs=[
            pl.BlockSpec((tm, d), lambda i: (i, 0)),
            pl.BlockSpec((None, None, 6, d), lambda i: (layer, i // per_b, 0, 0)),
            pl.BlockSpec((None, 1, d), lambda i: (layer, 0, 0)),
            pl.BlockSpec((None, 1, d), lambda i: (layer, 0, 0)),
            pl.BlockSpec((None, d, D_FF), lambda i: (layer, 0, 0)),
            pl.BlockSpec((None, d, D_FF), lambda i: (layer, 0, 1)),
            pl.BlockSpec((None, D_FF, d), lambda i: (layer, 0, 0)),
        ],
        out_specs=pl.BlockSpec((tm, d), lambda i: (i, 0)),
        out_shape=jax.ShapeDtypeStruct((t, d), F32),
        scratch_shapes=[pltpu.VMEM((tm, d), BF16), pltpu.VMEM((tm, d), F32)],
        compiler_params=pltpu.CompilerParams(
            dimension_semantics=("parallel",),
            vmem_limit_bytes=56 * 1024 * 1024),
        name="ffn",
    )(x2d, mod, g_pre, g_post, w_up, w_up, w_down)


def _pad_last(w, n):
    return jnp.pad(w, [(0, 0)] * (w.ndim - 1) + [(0, n - w.shape[-1])])


def _pad_rows(w, n):
    return jnp.pad(w, ((0, 0), (0, n - w.shape[1]), (0, 0)))


def _layout_w_in(w_in, b_gate):
    sizes = [W_SB] * 3 + [W_MOBA] * 3 + [W_DSA] * 3 + [IDX_HEADS * IDX_DIM, IDX_DIM, IDX_HEADS, N_BRANCH * D_MODEL]
    points = [int(v) for v in np.cumsum(sizes)[:-1]]
    scale = HEAD_DIM ** -0.5 * LOG2E
    col_scale = np.ones((sum(sizes),), np.float32)
    for q_seg in (0, 3, 6):
        col_scale[sum(sizes[:q_seg]):sum(sizes[:q_seg + 1])] = scale
    w16 = (w_in * col_scale).astype(BF16)
    (q_sb, k_sb, v_sb, q_mb, k_mb, v_mb, q_ds, k_ds, v_ds, q_ix, k_ix, w_ix, gates) = jnp.split(w16, points, axis=-1)
    cols = [q_sb, k_sb, v_sb,
            _pad_last(q_mb, PAIR_W), _pad_last(k_mb, PAIR_W), _pad_last(v_mb, PAIR_W),
            _pad_last(q_ds, PAIR_W), _pad_last(k_ds, PAIR_W), _pad_last(v_ds, PAIR_W),
            _pad_last(jnp.concatenate([k_ix, w_ix], axis=-1), LANES), q_ix, gates]
    w_all = jnp.concatenate(cols, axis=-1)
    bias = jnp.concatenate([jnp.zeros((b_gate.shape[0], COL_GATE), F32), b_gate], axis=-1)[:, None, :]
    return w_all, bias


def kernel(x, c, w_ada, b_ada, g_pre_mix, g_post_mix, w_in, b_gate, w_proj_sb, w_proj_moba,
           w_proj_dsa, w_o, g_pre_ffn, g_post_ffn, w_up, w_down):
    bsz, seq, d = x.shape
    depth = w_ada.shape[0]
    assert d == D_MODEL and seq % QB == 0 and QB == MOBA_BLOCK and KC == QB
    mod = _ada(c, w_ada, b_ada).reshape(depth, bsz, 6, d)
    w_all, bias = _layout_w_in(w_in, b_gate)
    assert w_all.shape[-1] == N_PROJ
    w_sb = w_proj_sb.astype(BF16)
    w_mb = _pad_rows(w_proj_moba, PAIR_W).astype(BF16)
    w_ds = _pad_rows(w_proj_dsa, PAIR_W).astype(BF16)
    w_o, w_up, w_down = w_o.astype(BF16), w_up.astype(BF16), w_down.astype(BF16)
    rows = lambda g: g[:, None, :]
    x2d = x.reshape(bsz * seq, d)
    for layer in range(depth):
        proj = _inproj(x2d, mod, rows(g_pre_mix), w_all, bias, layer, seq)
        proj3 = proj.reshape(bsz, seq, N_PROJ)
        o_sb = _sb_attention(proj3)
        o_mb = _moba_attention(proj3)
        o_ds = _dsa_attention(proj3)
        x2d = _merge(o_sb, o_mb, o_ds, proj, x2d, mod, rows(g_post_mix), w_sb, w_mb, w_ds, w_o, layer, seq)
        x2d = _ffn(x2d, mod, rows(g_pre_ffn), rows(g_post_ffn), w_up, w_down, layer, seq)
    return x2d.reshape(bsz, seq, d)
```

```python
import functools

import numpy as np
import jax
import jax.numpy as jnp
from jax import lax
from jax.experimental import pallas as pl
from jax.experimental.pallas import tpu as pltpu

F32 = jnp.float32
BF16 = jnp.bfloat16
I32 = jnp.int32

D_MODEL = 1024
HEAD_DIM = 64
H_SB, H_MOBA, H_DSA = 6, 5, 5
W_SB, W_MOBA, W_DSA = H_SB * HEAD_DIM, H_MOBA * HEAD_DIM, H_DSA * HEAD_DIM
MOBA_BLOCK = 256
MOBA_TOPK = 3
DSA_TOPK_MAX = 256
IDX_HEADS = 8
IDX_DIM = 64
D_FF = 2816
N_BRANCH = 3
RMS_EPS = 1e-6
NEG_BIG = -1e30
ALIBI_HEADS = H_MOBA + H_DSA

LANES = 128
PAIR_W = 3 * LANES
QB = 256
KC = 256
INT_MIN = -(2 ** 31)

COL_KX = 9 * PAIR_W
COL_QX = COL_KX + LANES
COL_GATE = COL_QX + IDX_HEADS * IDX_DIM
N_PROJ = COL_GATE + N_BRANCH * D_MODEL
PROJ_TN = 1024
FFN_TF = 256

NT = (((1,), (1,)), ((), ()))
TN = (((0,), (0,)), ((), ()))

LOG2E = 1.4426950408889634
_ALIBI = [float(2.0 ** (-8.0 * h / ALIBI_HEADS)) * LOG2E for h in range(1, ALIBI_HEADS + 1)]
SLOPES_MOBA = _ALIBI[0::2]
SLOPES_DSA = _ALIBI[1::2]


def _rms(x):
    return x * lax.rsqrt(jnp.mean(x * x, axis=-1, keepdims=True) + RMS_EPS)


def _fori_by_two(n, body, init):
    def two(i, state):
        return body(2 * i + 1, body(2 * i, state))
    state = lax.fori_loop(0, n // 2, two, init)
    return lax.fori_loop(2 * (n // 2), n, body, state)


def _half_mask(shape, lane_axis, hh):
    lane = lax.broadcasted_iota(I32, shape, lane_axis)
    return (lane < HEAD_DIM) if hh == 0 else (lane >= HEAD_DIM)


def _ada_kernel(c_ref, w_ref, b_ref, o_ref):
    c = c_ref[...]
    ca = c * jax.nn.sigmoid(c)
    o_ref[...] = jnp.dot(ca, w_ref[...], preferred_element_type=F32) + b_ref[...]


def _ada(c, w_ada, b_ada):
    depth, d, n = w_ada.shape
    bsz = c.shape[0]
    tn = 512
    return pl.pallas_call(
        _ada_kernel,
        grid=(depth, n // tn),
        in_specs=[
            pl.BlockSpec((bsz, d), lambda l, j: (0, 0)),
            pl.BlockSpec((None, d, tn), lambda l, j: (l, 0, j)),
            pl.BlockSpec((None, 1, tn), lambda l, j: (l, 0, j)),
        ],
        out_specs=pl.BlockSpec((None, bsz, tn), lambda l, j: (l, 0, j)),
        out_shape=jax.ShapeDtypeStruct((depth, bsz, n), F32),
        name="ada_mod",
    )(c, w_ada, b_ada.reshape(depth, 1, n))


def _inproj_kernel(x_ref, mod_ref, g_ref, w_ref, b_ref, o_ref):
    xh = _rms(x_ref[...])
    h = ((xh * g_ref[...]) * (1.0 + mod_ref[1:2, :]) + mod_ref[0:1, :]).astype(BF16)
    for n0 in range(0, N_PROJ, PROJ_TN):
        cols = slice(n0, n0 + PROJ_TN)
        acc = jnp.dot(h, w_ref[:, cols], preferred_element_type=F32) + b_ref[:, cols]
        if n0 >= COL_GATE:
            acc = jax.nn.sigmoid(acc)
        o_ref[:, cols] = acc.astype(BF16)


def _inproj(x2d, mod, g_pre, w_all, bias, layer, seq):
    t, d = x2d.shape
    tm = 512
    n = w_all.shape[-1]
    per_b = seq // tm
    return pl.pallas_call(
        _inproj_kernel,
        grid=(t // tm,),
        in_specs=[
            pl.BlockSpec((tm, d), lambda i: (i, 0)),
            pl.BlockSpec((None, None, 6, d), lambda i: (layer, i // per_b, 0, 0)),
            pl.BlockSpec((None, 1, d), lambda i: (layer, 0, 0)),
            pl.BlockSpec((None, d, n), lambda i: (layer, 0, 0)),
            pl.BlockSpec((None, 1, n), lambda i: (layer, 0, 0)),
        ],
        out_specs=pl.BlockSpec((tm, n), lambda i: (i, 0)),
        out_shape=jax.ShapeDtypeStruct((t, n), BF16),
        compiler_params=pltpu.CompilerParams(
            dimension_semantics=("parallel",),
            vmem_limit_bytes=56 * 1024 * 1024),
        name="in_proj",
    )(x2d, mod, g_pre, w_all, bias)


def _masked_heads(ref, dst_ref, n_heads):
    for head in range(n_heads):
        grp = ref[:, (head // 2) * LANES:(head // 2 + 1) * LANES]
        dst_ref[head] = jnp.where(_half_mask(grp.shape, 1, head % 2), grp, jnp.zeros_like(grp))


def _sb_kernel(q_ref, k_ref, v_ref, o_ref, qh_ref, lb_ref, l1m_ref, carry_ref, acc_ref):
    qi = pl.program_id(1)
    row = lax.broadcasted_iota(I32, (KC, QB), 0)
    lane = lax.broadcasted_iota(I32, (KC, QB), 1)
    past_diag = row < lane
    upper = (lax.broadcasted_iota(I32, (KC, KC), 1) > lax.broadcasted_iota(I32, (KC, KC), 0)).astype(BF16)

    _masked_heads(q_ref, qh_ref, H_SB)

    def terms(c, carries, diag):
        start = pl.multiple_of(c * KC, KC)
        new = list(carries)
        for p in range(H_SB // 2):
            kc = k_ref[pl.ds(start, KC), p * LANES:(p + 1) * LANES]
            for head in (2 * p, 2 * p + 1):
                z = lax.dot_general(kc, qh_ref[head], NT, preferred_element_type=F32)
                log_beta = jnp.minimum(z, 0.0) - jnp.log2(1.0 + jnp.exp2(-jnp.abs(z)))
                log_1m = log_beta - z
                if diag:
                    log_1m = jnp.where(past_diag, log_1m, 0.0)
                    log_beta = jnp.where(past_diag, log_beta, NEG_BIG)
                lb_ref[head, c] = log_beta
                l1m_ref[head, c] = log_1m.astype(BF16)
                carry_ref[head, c] = carries[head]
                new[head] = carries[head] + _fold8(log_1m, jnp.sum)
        return tuple(new)

    def attend(c):
        start = pl.multiple_of(c * KC, KC)
        for p in range(H_SB // 2):
            vc = v_ref[pl.ds(start, KC), p * LANES:(p + 1) * LANES]
            pv = None
            for head in (2 * p, 2 * p + 1):
                between = jnp.dot(upper, l1m_ref[head, c], preferred_element_type=F32)
                carry = jnp.sum(carry_ref[head, c], axis=0, keepdims=True)
                a = jnp.exp2(lb_ref[head, c] + between + carry)
                vh = jnp.where(_half_mask(vc.shape, 1, head % 2), vc, jnp.zeros_like(vc))
                part = lax.dot_general(vh, a.astype(BF16), TN, preferred_element_type=F32)
                pv = part if pv is None else pv + part
            acc_ref[p] += pv

    def pipelined(i, carries):
        c = qi - 1 - i
        attend(c + 1)
        return terms(c, carries, False)

    acc_ref[...] = jnp.zeros_like(acc_ref)
    carries = terms(qi, tuple(jnp.zeros((8, QB), F32) for _ in range(H_SB)), True)
    _fori_by_two(qi, pipelined, carries)
    attend(0)
    for p in range(H_SB // 2):
        o_ref[p * LANES:(p + 1) * LANES, :] = acc_ref[p].astype(BF16)


def _sb_attention(proj3):
    bsz, seq, _ = proj3.shape
    return pl.pallas_call(
        _sb_kernel,
        grid=(bsz, seq // QB),
        in_specs=[
            pl.BlockSpec((None, QB, PAIR_W), lambda b, i: (b, i, 0)),
            pl.BlockSpec((None, seq, PAIR_W), lambda b, i: (b, 0, 1)),
            pl.BlockSpec((None, seq, PAIR_W), lambda b, i: (b, 0, 2)),
        ],
        out_specs=pl.BlockSpec((None, PAIR_W, QB), lambda b, i: (b, 0, i)),
        out_shape=jax.ShapeDtypeStruct((bsz, PAIR_W, seq), BF16),
        scratch_shapes=[pltpu.VMEM((H_SB, QB, LANES), BF16),
                        pltpu.VMEM((H_SB, seq // KC, KC, QB), F32),
                        pltpu.VMEM((H_SB, seq // KC, KC, QB), BF16),
                        pltpu.VMEM((H_SB, seq // KC, 8, QB), F32),
                        pltpu.VMEM((H_SB // 2, LANES, QB), F32)],
        compiler_params=pltpu.CompilerParams(
            dimension_semantics=("parallel", "arbitrary"),
            vmem_limit_bytes=48 * 1024 * 1024),
        name="sb_attn",
    )(proj3, proj3, proj3)


def _fold8(x, op):
    return op(x.reshape(KC // 8, 8, QB), axis=0)


def _softmax_heads(qh_ref, k_ref, v_ref, s_ref, acc_ref, o_ref, qi, n_heads, table, shared_bias, shift):
    out_row = lax.broadcasted_iota(I32, (LANES, QB), 0)
    n_pairs = (n_heads + 1) // 2
    pair_heads = [[h for h in (2 * p, 2 * p + 1) if h < n_heads] for p in range(n_pairs)]

    def by_head_rows(vals):
        return vals[0] if len(vals) == 1 else jnp.where(out_row < HEAD_DIM, vals[0], vals[1])

    def score(c, maxes, diag):
        start = pl.multiple_of(c * KC, KC)
        shared = None if shared_bias is None else shared_bias(c, diag)
        new = list(maxes)
        for p in range(n_pairs):
            kc = k_ref[pl.ds(start, KC), p * LANES:(p + 1) * LANES]
            for head in pair_heads[p]:
                s = lax.dot_general(kc, qh_ref[head], NT, preferred_element_type=F32) + table(head, diag)
                if shared is not None:
                    s = s + shared
                s_ref[head, c] = s
                new[head] = jnp.maximum(maxes[head], _fold8(s, jnp.max) + shift(c, head))
        return tuple(new)

    maxes = score(qi, tuple(jnp.full((8, QB), NEG_BIG, F32) for _ in range(n_heads)), True)
    maxes = _fori_by_two(qi, lambda c, mx: score(c, mx, False), maxes)
    m = [jnp.max(mx, axis=0, keepdims=True) for mx in maxes]

    acc_ref[...] = jnp.zeros_like(acc_ref)

    def attend(c, sums):
        start = pl.multiple_of(c * KC, KC)
        new = list(sums)
        for p in range(n_pairs):
            vc = v_ref[pl.ds(start, KC), p * LANES:(p + 1) * LANES]
            pv = None
            for head in pair_heads[p]:
                pr = jnp.exp2(s_ref[head, c] + (shift(c, head) - m[head]))
                new[head] = sums[head] + _fold8(pr, jnp.sum)
                vh = vc if len(pair_heads[p]) == 1 else jnp.where(
                    _half_mask(vc.shape, 1, head % 2), vc, jnp.zeros_like(vc))
                part = lax.dot_general(vh, pr.astype(BF16), TN, preferred_element_type=F32)
                pv = part if pv is None else pv + part
            acc_ref[p] += pv
        return tuple(new)

    sums = _fori_by_two(qi + 1, attend, tuple(jnp.zeros((8, QB), F32) for _ in range(n_heads)))
    for p in range(n_pairs):
        denom = by_head_rows([jnp.sum(sums[h], axis=0, keepdims=True) for h in pair_heads[p]])
        o_ref[p * LANES:(p + 1) * LANES, :] = (acc_ref[p] / denom).astype(BF16)


def _alibi_tables(table_ref, slopes, causal):
    row = lax.broadcasted_iota(I32, (KC, QB), 0)
    lane = lax.broadcasted_iota(I32, (KC, QB), 1)
    key_minus_query = (row - lane).astype(F32)
    for head, slope in enumerate(slopes):
        table_ref[head, 0] = slope * key_minus_query
        if causal:
            table_ref[head, 1] = jnp.where(row <= lane, slope * key_minus_query, NEG_BIG)


def _moba_kernel(q_ref, k_ref, v_ref, o_ref, kmean_ref, sel_ref, qh_ref, table_ref, s_ref, acc_ref):
    qi = pl.program_id(1)
    n_blocks = kmean_ref.shape[0]

    @pl.when(qi == 0)
    def _():
        for n in range(n_blocks):
            kb = k_ref[n * MOBA_BLOCK:(n + 1) * MOBA_BLOCK, :].astype(F32)
            kmean_ref[n:n + 1, :] = jnp.mean(kb, axis=0, keepdims=True)
        _alibi_tables(table_ref, SLOPES_MOBA, causal=True)

    blk = lax.broadcasted_iota(I32, (n_blocks, QB), 0)

    _masked_heads(q_ref, qh_ref, H_MOBA)

    for head in range(H_MOBA):
        cols = slice((head // 2) * LANES, (head // 2 + 1) * LANES)
        gate = lax.dot_general(kmean_ref[:, cols], qh_ref[head].astype(F32), NT,
                               preferred_element_type=F32, precision=lax.Precision.HIGHEST)
        rank = jnp.zeros((n_blocks, QB), I32)
        for mb in range(n_blocks):
            gm = gate[mb:mb + 1, :]
            beats = (gm > gate) | ((gm == gate) & (mb < blk))
            rank = rank + jnp.where(beats, (mb < qi).astype(I32), 0)
        selected = ((rank < MOBA_TOPK) & (blk < qi)) | (blk == qi)
        sel_ref[head] = jnp.where(selected, 0.0, NEG_BIG)

    def shift(c, head):
        return sel_ref[head, pl.ds(c, 1), :] + SLOPES_MOBA[head] * ((c - qi) * KC).astype(F32)

    _softmax_heads(qh_ref, k_ref, v_ref, s_ref, acc_ref, o_ref, qi, H_MOBA,
                   table=lambda head, diag: table_ref[head, 1 if diag else 0],
                   shared_bias=None, shift=shift)


def _moba_attention(proj3):
    bsz, seq, _ = proj3.shape
    n_blocks = seq // MOBA_BLOCK
    return pl.pallas_call(
        _moba_kernel,
        grid=(bsz, seq // QB),
        in_specs=[
            pl.BlockSpec((None, QB, PAIR_W), lambda b, i: (b, i, 3)),
            pl.BlockSpec((None, seq, PAIR_W), lambda b, i: (b, 0, 4)),
            pl.BlockSpec((None, seq, PAIR_W), lambda b, i: (b, 0, 5)),
        ],
        out_specs=pl.BlockSpec((None, PAIR_W, QB), lambda b, i: (b, 0, i)),
        out_shape=jax.ShapeDtypeStruct((bsz, PAIR_W, seq), BF16),
        scratch_shapes=[pltpu.VMEM((n_blocks, PAIR_W), F32),
                        pltpu.VMEM((H_MOBA, n_blocks, QB), F32),
                        pltpu.VMEM((H_MOBA, QB, LANES), BF16),
                        pltpu.VMEM((H_MOBA, 2, KC, QB), F32),
                        pltpu.VMEM((H_MOBA, seq // KC, KC, QB), F32),
                        pltpu.VMEM((PAIR_W // LANES, LANES, QB), F32)],
        compiler_params=pltpu.CompilerParams(
            dimension_semantics=("parallel", "arbitrary"),
            vmem_limit_bytes=48 * 1024 * 1024),
        name="moba_attn",
    )(proj3, proj3, proj3)


def _dsa_kernel(q_ref, k_ref, v_ref, qx_ref, kxq_ref, kx_ref, o_ref,
                kx2_ref, score_ref, tau_ref, cidx_ref, qh_ref, table_ref, s_ref, acc_ref, *, top):
    qi = pl.program_id(1)
    seq = k_ref.shape[0]

    @pl.when(qi == 0)
    def _():
        kx = kx_ref[...].astype(F32)
        dup = jnp.where(_half_mask(kx.shape, 1, 0), kx, pltpu.roll(kx, HEAD_DIM, 1))
        kx2_ref[...] = dup.astype(BF16)
        _alibi_tables(table_ref, SLOPES_DSA, causal=False)

    row = lax.broadcasted_iota(I32, (KC, QB), 0)
    lane = lax.broadcasted_iota(I32, (KC, QB), 1)
    causal_diag = row <= lane

    pick = (lax.broadcasted_iota(I32, (IDX_HEADS, LANES), 1)
            == lax.broadcasted_iota(I32, (IDX_HEADS, LANES), 0) + IDX_DIM).astype(BF16)
    w_t = lax.dot_general(pick, kxq_ref[...], NT, preferred_element_type=F32)

    def score_rows(c, size, diag):
        start = pl.multiple_of(c * KC, KC)
        kc = kx2_ref[pl.ds(start, size), :]
        sc = jnp.zeros((size, QB), F32)
        for pp in range(IDX_HEADS // 2):
            qp = qx_ref[:, pp * LANES:(pp + 1) * LANES]
            for hh in range(2):
                h = 2 * pp + hh
                qh = jnp.where(_half_mask(qp.shape, 1, hh), qp, jnp.zeros_like(qp))
                lg = lax.dot_general(kc, qh, NT, preferred_element_type=F32)
                sc = sc + jnp.maximum(lg, 0.0) * w_t[h:h + 1, :]
        if diag:
            sc = jnp.where(causal_diag, sc, NEG_BIG)
        score_ref[pl.ds(start, size), :] = sc

    score_rows(qi, KC, True)

    def _score_two(i, carry):
        score_rows(2 * i, 2 * KC, False)
        return carry

    def _score_one(c, carry):
        score_rows(c, KC, False)
        return carry

    lax.fori_loop(0, qi // 2, _score_two, 0)
    lax.fori_loop(2 * (qi // 2), qi, _score_one, 0)

    def count(n, pred):
        accs = [jnp.zeros((8, QB), I32) for _ in range(4)]
        for c in range(n):
            hit = jnp.where(pred(score_ref[c * KC:(c + 1) * KC, :], c), 1, 0).reshape(KC // 8, 8, QB)
            for r in range(KC // 8):
                accs[r % 4] = accs[r % 4] + hit[r]
        return jnp.sum((accs[0] + accs[1]) + (accs[2] + accs[3]), axis=0, keepdims=True)

    def ordered_to_float(key):
        return pltpu.bitcast(jnp.where(key < 0, key ^ jnp.int32(0x7FFFFFFF), key), F32)

    def find_threshold(n):
        def bit_step(i, state):
            key, cnt_key = state
            cand_key = key + lax.shift_left(jnp.int32(1), 31 - i)
            cand = ordered_to_float(cand_key)
            cnt = count(n, lambda s, c: s >= cand)
            ok = cnt >= top
            return jnp.where(ok, cand_key, key), jnp.where(ok, cnt, cnt_key)

        start = (jnp.full((1, QB), INT_MIN, I32), jnp.full((1, QB), n * KC, I32))
        key, at_least = lax.fori_loop(0, 32, bit_step, start)
        tau = ordered_to_float(key)
        tau_ref[...] = jnp.broadcast_to(tau, tau_ref.shape)

        @pl.when(jnp.max(at_least) > top)
        def _():
            need = top - count(n, lambda s, c: s > tau)

            def idx_step(i, x):
                cand = x + lax.shift_left(jnp.int32(1), 11 - i)
                cnt = count(n, lambda s, c: (s == tau) & ((c * KC + row) < cand))
                return jnp.where(cnt < need, cand, x)
            x = lax.fori_loop(0, 12, idx_step, jnp.zeros((1, QB), I32))
            cidx_ref[...] = jnp.broadcast_to(x, cidx_ref.shape)

    tau_ref[...] = jnp.full(tau_ref.shape, -jnp.inf, F32)
    cidx_ref[...] = jnp.full(cidx_ref.shape, seq, I32)
    for n in range(2, seq // KC + 1):
        pl.when(qi == n - 1)(functools.partial(find_threshold, n))

    tau = tau_ref[0:1, :]
    cidx = cidx_ref[0:1, :]

    def keep_bias(c, diag):
        scores = score_ref[pl.ds(pl.multiple_of(c * KC, KC), KC), :]
        keep = (scores > tau) | ((scores == tau) & ((c * KC + row) <= cidx))
        if diag:
            keep = keep & causal_diag
        return jnp.where(keep, 0.0, NEG_BIG)

    _masked_heads(q_ref, qh_ref, H_DSA)

    _softmax_heads(qh_ref, k_ref, v_ref, s_ref, acc_ref, o_ref, qi, H_DSA,
                   table=lambda head, diag: table_ref[head, 0],
                   shared_bias=keep_bias,
                   shift=lambda c, head: SLOPES_DSA[head] * ((c - qi) * KC).astype(F32))


def _dsa_attention(proj3):
    bsz, seq, _ = proj3.shape
    top = min(DSA_TOPK_MAX, seq // 4)
    assert top == QB, "the first query block must keep every admissible key"
    return pl.pallas_call(
        functools.partial(_dsa_kernel, top=top),
        grid=(bsz, seq // QB),
        in_specs=[
            pl.BlockSpec((None, QB, PAIR_W), lambda b, i: (b, i, 6)),
            pl.BlockSpec((None, seq, PAIR_W), lambda b, i: (b, 0, 7)),
            pl.BlockSpec((None, seq, PAIR_W), lambda b, i: (b, 0, 8)),
            pl.BlockSpec((None, QB, IDX_HEADS * IDX_DIM), lambda b, i: (b, i, COL_QX // (IDX_HEADS * IDX_DIM))),
            pl.BlockSpec((None, QB, LANES), lambda b, i: (b, i, COL_KX // LANES)),
            pl.BlockSpec((None, seq, LANES), lambda b, i: (b, 0, COL_KX // LANES)),
        ],
        out_specs=pl.BlockSpec((None, PAIR_W, QB), lambda b, i: (b, 0, i)),
        out_shape=jax.ShapeDtypeStruct((bsz, PAIR_W, seq), BF16),
        scratch_shapes=[pltpu.VMEM((seq, LANES), BF16),
                        pltpu.VMEM((seq, QB), F32),
                        pltpu.VMEM((8, QB), F32),
                        pltpu.VMEM((8, QB), I32),
                        pltpu.VMEM((H_DSA, QB, LANES), BF16),
                        pltpu.VMEM((H_DSA, 1, KC, QB), F32),
                        pltpu.VMEM((H_DSA, seq // KC, KC, QB), F32),
                        pltpu.VMEM((PAIR_W // LANES, LANES, QB), F32)],
        compiler_params=pltpu.CompilerParams(
            dimension_semantics=("parallel", "arbitrary"),
            vmem_limit_bytes=48 * 1024 * 1024),
        name="dsa_attn",
    )(proj3, proj3, proj3, proj3, proj3, proj3)


def _merge_kernel(osb_ref, omb_ref, ods_ref, gsb_ref, gmb_ref, gds_ref, x_ref, mod_ref, gpost_ref,
                  wsb_ref, wmb_ref, wds_ref, wo_ref, out_ref):
    def branch(o_ref, w_ref, g_ref):
        y = lax.dot_general(o_ref[...], w_ref[...], TN, preferred_element_type=F32)
        return g_ref[...].astype(F32) * y

    merged = branch(osb_ref, wsb_ref, gsb_ref) + branch(omb_ref, wmb_ref, gmb_ref) + branch(ods_ref, wds_ref, gds_ref)
    y = jnp.dot(merged.astype(BF16), wo_ref[...], preferred_element_type=F32)
    out_ref[...] = x_ref[...] + mod_ref[2:3, :] * (_rms(y) * gpost_ref[...])


def _merge(o_sb, o_mb, o_ds, proj, x2d, mod, g_post, w_sb, w_mb, w_ds, w_o, layer, seq):
    t, d = x2d.shape
    tm = 1024
    per_b = seq // tm
    gate_blk = COL_GATE // d
    o_spec = pl.BlockSpec((None, PAIR_W, tm), lambda i: (i // per_b, 0, i % per_b))
    w_spec = pl.BlockSpec((None, PAIR_W, d), lambda i: (layer, 0, 0))
    return pl.pallas_call(
        _merge_kernel,
        grid=(t // tm,),
        in_specs=[
            o_spec, o_spec, o_spec,
            pl.BlockSpec((tm, d), lambda i: (i, gate_blk)),
            pl.BlockSpec((tm, d), lambda i: (i, gate_blk + 1)),
            pl.BlockSpec((tm, d), lambda i: (i, gate_blk + 2)),
            pl.BlockSpec((tm, d), lambda i: (i, 0)),
            pl.BlockSpec((None, None, 6, d), lambda i: (layer, i // per_b, 0, 0)),
            pl.BlockSpec((None, 1, d), lambda i: (layer, 0, 0)),
            w_spec, w_spec, w_spec,
            pl.BlockSpec((None, d, d), lambda i: (layer, 0, 0)),
        ],
        out_specs=pl.BlockSpec((tm, d), lambda i: (i, 0)),
        out_shape=jax.ShapeDtypeStruct((t, d), F32),
        compiler_params=pltpu.CompilerParams(
            dimension_semantics=("parallel",),
            vmem_limit_bytes=48 * 1024 * 1024),
        name="merge_out",
    )(o_sb, o_mb, o_ds, proj, proj, proj, x2d, mod, g_post, w_sb, w_mb, w_ds, w_o)


def _ffn_kernel(x_ref, mod_ref, gpre_ref, gpost_ref, wg_ref, wu_ref, wd_ref, out_ref, h_ref, acc_ref):
    xh = _rms(x_ref[...])
    h_ref[...] = ((xh * gpre_ref[...]) * (1.0 + mod_ref[4:5, :]) + mod_ref[3:4, :]).astype(BF16)
    for f in range(0, D_FF, FFN_TF):
        h = h_ref[...]
        gate = jnp.dot(h, wg_ref[:, f:f + FFN_TF], preferred_element_type=F32)
        up = jnp.dot(h, wu_ref[:, f:f + FFN_TF], preferred_element_type=F32)
        act = ((gate * jax.nn.sigmoid(gate)) * up).astype(BF16)
        part = jnp.dot(act, wd_ref[f:f + FFN_TF, :], preferred_element_type=F32)
        if f == 0:
            acc_ref[...] = part
        else:
            acc_ref[...] += part
    y = acc_ref[...]
    out_ref[...] = x_ref[...] + mod_ref[5:6, :] * (_rms(y) * gpost_ref[...])


def _ffn(x2d, mod, g_pre, g_post, w_up, w_down, layer, seq):
    t, d = x2d.shape
    tm = 512
    per_b = seq // tm
    return pl.pallas_call(
        _ffn_kernel,
        grid=(t // tm,),
        in_specs=[
            pl.BlockSpec((tm, d), lambda i: (i, 0)),
            pl.BlockSpec((None, None, 6, d), lambda i: (layer, i // per_b, 0, 0)),
            pl.BlockSpec((None, 1, d), lambda i: (layer, 0, 0)),
            pl.BlockSpec((None, 1, d), lambda i: (layer, 0, 0)),
            pl.BlockSpec((None, d, D_FF), lambda i: (layer, 0, 0)),
            pl.BlockSpec((None, d, D_FF), lambda i: (layer, 0, 1)),
            pl.BlockSpec((None, D_FF, d), lambda i: (layer, 0, 0)),
        ],
        out_specs=pl.BlockSpec((tm, d), lambda i: (i, 0)),
        out_shape=jax.ShapeDtypeStruct((t, d), F32),
        scratch_shapes=[pltpu.VMEM((tm, d), BF16), pltpu.VMEM((tm, d), F32)],
        compiler_params=pltpu.CompilerParams(
            dimension_semantics=("parallel",),
            vmem_limit_bytes=56 * 1024 * 1024),
        name="ffn",
    )(x2d, mod, g_pre, g_post, w_up, w_up, w_down)


def _pad_last(w, n):
    return jnp.pad(w, [(0, 0)] * (w.ndim - 1) + [(0, n - w.shape[-1])])


def _pad_rows(w, n):
    return jnp.pad(w, ((0, 0), (0, n - w.shape[1]), (0, 0)))


def _layout_w_in(w_in, b_gate):
    sizes = [W_SB] * 3 + [W_MOBA] * 3 + [W_DSA] * 3 + [IDX_HEADS * IDX_DIM, IDX_DIM, IDX_HEADS, N_BRANCH * D_MODEL]
    points = [int(v) for v in np.cumsum(sizes)[:-1]]
    scale = HEAD_DIM ** -0.5 * LOG2E
    col_scale = np.ones((sum(sizes),), np.float32)
    for q_seg in (0, 3, 6):
        col_scale[sum(sizes[:q_seg]):sum(sizes[:q_seg + 1])] = scale
    w16 = (w_in * col_scale).astype(BF16)
    (q_sb, k_sb, v_sb, q_mb, k_mb, v_mb, q_ds, k_ds, v_ds, q_ix, k_ix, w_ix, gates) = jnp.split(w16, points, axis=-1)
    cols = [q_sb, k_sb, v_sb,
            _pad_last(q_mb, PAIR_W), _pad_last(k_mb, PAIR_W), _pad_last(v_mb, PAIR_W),
            _pad_last(q_ds, PAIR_W), _pad_last(k_ds, PAIR_W), _pad_last(v_ds, PAIR_W),
            _pad_last(jnp.concatenate([k_ix, w_ix], axis=-1), LANES), q_ix, gates]
    w_all = jnp.concatenate(cols, axis=-1)
    bias = jnp.concatenate([jnp.zeros((b_gate.shape[0], COL_GATE), F32), b_gate], axis=-1)[:, None, :]
    return w_all, bias


def kernel(x, c, w_ada, b_ada, g_pre_mix, g_post_mix, w_in, b_gate, w_proj_sb, w_proj_moba,
           w_proj_dsa, w_o, g_pre_ffn, g_post_ffn, w_up, w_down):
    bsz, seq, d = x.shape
    depth = w_ada.shape[0]
    assert d == D_MODEL and seq % QB == 0 and QB == MOBA_BLOCK and KC == QB
    mod = _ada(c, w_ada, b_ada).reshape(depth, bsz, 6, d)
    w_all, bias = _layout_w_in(w_in, b_gate)
    assert w_all.shape[-1] == N_PROJ
    w_sb = w_proj_sb.astype(BF16)
    w_mb = _pad_rows(w_proj_moba, PAIR_W).astype(BF16)
    w_ds = _pad_rows(w_proj_dsa, PAIR_W).astype(BF16)
    w_o, w_up, w_down = w_o.astype(BF16), w_up.astype(BF16), w_down.astype(BF16)
    rows = lambda g: g[:, None, :]
    x2d = x.reshape(bsz * seq, d)
    for layer in range(depth):
        proj = _inproj(x2d, mod, rows(g_pre_mix), w_all, bias, layer, seq)
        proj3 = proj.reshape(bsz, seq, N_PROJ)
        o_sb = _sb_attention(proj3)
        o_mb = _moba_attention(proj3)
        o_ds = _dsa_attention(proj3)
        x2d = _merge(o_sb, o_mb, o_ds, proj, x2d, mod, rows(g_post_mix), w_sb, w_mb, w_ds, w_o, layer, seq)
        x2d = _ffn(x2d, mod, rows(g_pre_ffn), rows(g_post_ffn), w_up, w_down, layer, seq)
    return x2d.reshape(bsz, seq, d)
```

```python
import functools

import numpy as np
import jax
import jax.numpy as jnp
from jax import lax
from jax.experimental import pallas as pl
from jax.experimental.pallas import tpu as pltpu

F32 = jnp.float32
BF16 = jnp.bfloat16
I32 = jnp.int32

D_MODEL = 1024
HEAD_DIM = 64
H_SB, H_MOBA, H_DSA = 6, 5, 5
W_SB, W_MOBA, W_DSA = H_SB * HEAD_DIM, H_MOBA * HEAD_DIM, H_DSA * HEAD_DIM
MOBA_BLOCK = 256
MOBA_TOPK = 3
DSA_TOPK_MAX = 256
IDX_HEADS = 8
IDX_DIM = 64
D_FF = 2816
N_BRANCH = 3
RMS_EPS = 1e-6
NEG_BIG = -1e30
ALIBI_HEADS = H_MOBA + H_DSA

LANES = 128
PAIR_W = 3 * LANES
QB = 256
KC = 256
INT_MIN = -(2 ** 31)
BF16_ROWS = 16
FINE_BITS = 18

COL_KX = 9 * PAIR_W
COL_QX = COL_KX + LANES
COL_GATE = COL_QX + IDX_HEADS * IDX_DIM
N_PROJ = COL_GATE + N_BRANCH * D_MODEL
PROJ_TN = 1024
FFN_TF = 256

NT = (((1,), (1,)), ((), ()))
TN = (((0,), (0,)), ((), ()))

LOG2E = 1.4426950408889634
_ALIBI = [float(2.0 ** (-8.0 * h / ALIBI_HEADS)) * LOG2E for h in range(1, ALIBI_HEADS + 1)]
SLOPES_MOBA = _ALIBI[0::2]
SLOPES_DSA = _ALIBI[1::2]


def _rms(x):
    return x * lax.rsqrt(jnp.mean(x * x, axis=-1, keepdims=True) + RMS_EPS)


def _fori_by_two(n, body, init):
    def two(i, state):
        return body(2 * i + 1, body(2 * i, state))
    state = lax.fori_loop(0, n // 2, two, init)
    return lax.fori_loop(2 * (n // 2), n, body, state)


def _half_mask(shape, lane_axis, hh):
    lane = lax.broadcasted_iota(I32, shape, lane_axis)
    return (lane < HEAD_DIM) if hh == 0 else (lane >= HEAD_DIM)


def _ada_kernel(c_ref, w_ref, b_ref, o_ref):
    c = c_ref[...]
    ca = c * jax.nn.sigmoid(c)
    o_ref[...] = jnp.dot(ca, w_ref[...], preferred_element_type=F32) + b_ref[...]


def _ada(c, w_ada, b_ada):
    depth, d, n = w_ada.shape
    bsz = c.shape[0]
    tn = 512
    return pl.pallas_call(
        _ada_kernel,
        grid=(depth, n // tn),
        in_specs=[
            pl.BlockSpec((bsz, d), lambda l, j: (0, 0)),
            pl.BlockSpec((None, d, tn), lambda l, j: (l, 0, j)),
            pl.BlockSpec((None, 1, tn), lambda l, j: (l, 0, j)),
        ],
        out_specs=pl.BlockSpec((None, bsz, tn), lambda l, j: (l, 0, j)),
        out_shape=jax.ShapeDtypeStruct((depth, bsz, n), F32),
        name="ada_mod",
    )(c, w_ada, b_ada.reshape(depth, 1, n))


def _inproj_kernel(x_ref, mod_ref, g_ref, w_ref, b_ref, o_ref):
    xh = _rms(x_ref[...])
    h = ((xh * g_ref[...]) * (1.0 + mod_ref[1:2, :]) + mod_ref[0:1, :]).astype(BF16)
    for n0 in range(0, N_PROJ, PROJ_TN):
        cols = slice(n0, n0 + PROJ_TN)
        acc = jnp.dot(h, w_ref[:, cols], preferred_element_type=F32) + b_ref[:, cols]
        if n0 >= COL_GATE:
            acc = jax.nn.sigmoid(acc)
        o_ref[:, cols] = acc.astype(BF16)


def _inproj(x2d, mod, g_pre, w_all, bias, layer, seq):
    t, d = x2d.shape
    tm = 512
    n = w_all.shape[-1]
    per_b = seq // tm
    return pl.pallas_call(
        _inproj_kernel,
        grid=(t // tm,),
        in_specs=[
            pl.BlockSpec((tm, d), lambda i: (i, 0)),
            pl.BlockSpec((None, None, 6, d), lambda i: (layer, i // per_b, 0, 0)),
            pl.BlockSpec((None, 1, d), lambda i: (layer, 0, 0)),
            pl.BlockSpec((None, d, n), lambda i: (layer, 0, 0)),
            pl.BlockSpec((None, 1, n), lambda i: (layer, 0, 0)),
        ],
        out_specs=pl.BlockSpec((tm, n), lambda i: (i, 0)),
        out_shape=jax.ShapeDtypeStruct((t, n), BF16),
        compiler_params=pltpu.CompilerParams(
            dimension_semantics=("parallel",),
            vmem_limit_bytes=56 * 1024 * 1024),
        name="in_proj",
    )(x2d, mod, g_pre, w_all, bias)


def _masked_heads(ref, dst_ref, n_heads):
    for head in range(n_heads):
        grp = ref[:, (head // 2) * LANES:(head // 2 + 1) * LANES]
        dst_ref[head] = jnp.where(_half_mask(grp.shape, 1, head % 2), grp, jnp.zeros_like(grp))


def _sb_kernel(q_ref, k_ref, v_ref, o_ref, qh_ref, lb_ref, l1m_ref, carry_ref, acc_ref):
    qi = pl.program_id(1)
    row = lax.broadcasted_iota(I32, (KC, QB), 0)
    lane = lax.broadcasted_iota(I32, (KC, QB), 1)
    past_diag = row < lane
    upper = (lax.broadcasted_iota(I32, (KC, KC), 1) > lax.broadcasted_iota(I32, (KC, KC), 0)).astype(BF16)

    _masked_heads(q_ref, qh_ref, H_SB)

    def terms(c, carries, diag):
        start = pl.multiple_of(c * KC, KC)
        new = list(carries)
        for p in range(H_SB // 2):
            kc = k_ref[pl.ds(start, KC), p * LANES:(p + 1) * LANES]
            for head in (2 * p, 2 * p + 1):
                z = lax.dot_general(kc, qh_ref[head], NT, preferred_element_type=F32)
                log_beta = jnp.minimum(z, 0.0) - jnp.log2(1.0 + jnp.exp2(-jnp.abs(z)))
                log_1m = log_beta - z
                if diag:
                    log_1m = jnp.where(past_diag, log_1m, 0.0)
                    log_beta = jnp.where(past_diag, log_beta, NEG_BIG)
                lb_ref[head, c] = log_beta
                l1m_ref[head, c] = log_1m.astype(BF16)
                carry_ref[head, c] = carries[head]
                new[head] = carries[head] + _fold8(log_1m, jnp.sum)
        return tuple(new)

    def attend(c):
        start = pl.multiple_of(c * KC, KC)
        for p in range(H_SB // 2):
            vc = v_ref[pl.ds(start, KC), p * LANES:(p + 1) * LANES]
            pv = None
            for head in (2 * p, 2 * p + 1):
                between = jnp.dot(upper, l1m_ref[head, c], preferred_element_type=F32)
                carry = jnp.sum(carry_ref[head, c], axis=0, keepdims=True)
                a = jnp.exp2(lb_ref[head, c] + between + carry)
                vh = jnp.where(_half_mask(vc.shape, 1, head % 2), vc, jnp.zeros_like(vc))
                part = lax.dot_general(vh, a.astype(BF16), TN, preferred_element_type=F32)
                pv = part if pv is None else pv + part
            acc_ref[p] += pv

    def pipelined(i, carries):
        c = qi - 1 - i
        attend(c + 1)
        return terms(c, carries, False)

    acc_ref[...] = jnp.zeros_like(acc_ref)
    carries = terms(qi, tuple(jnp.zeros((8, QB), F32) for _ in range(H_SB)), True)
    _fori_by_two(qi, pipelined, carries)
    attend(0)
    for p in range(H_SB // 2):
        o_ref[p * LANES:(p + 1) * LANES, :] = acc_ref[p].astype(BF16)


def _sb_attention(proj3):
    bsz, seq, _ = proj3.shape
    return pl.pallas_call(
        _sb_kernel,
        grid=(bsz, seq // QB),
        in_specs=[
            pl.BlockSpec((None, QB, PAIR_W), lambda b, i: (b, i, 0)),
            pl.BlockSpec((None, seq, PAIR_W), lambda b, i: (b, 0, 1)),
            pl.BlockSpec((None, seq, PAIR_W), lambda b, i: (b, 0, 2)),
        ],
        out_specs=pl.BlockSpec((None, PAIR_W, QB), lambda b, i: (b, 0, i)),
        out_shape=jax.ShapeDtypeStruct((bsz, PAIR_W, seq), BF16),
        scratch_shapes=[pltpu.VMEM((H_SB, QB, LANES), BF16),
                        pltpu.VMEM((H_SB, seq // KC, KC, QB), F32),
                        pltpu.VMEM((H_SB, seq // KC, KC, QB), BF16),
                        pltpu.VMEM((H_SB, seq // KC, 8, QB), F32),
                        pltpu.VMEM((H_SB // 2, LANES, QB), F32)],
        compiler_params=pltpu.CompilerParams(
            dimension_semantics=("parallel", "arbitrary"),
            vmem_limit_bytes=48 * 1024 * 1024),
        name="sb_attn",
    )(proj3, proj3, proj3)


def _fold8(x, op):
    return op(x.reshape(KC // 8, 8, QB), axis=0)


def _softmax_heads(qh_ref, k_ref, v_ref, s_ref, acc_ref, o_ref, qi, n_heads, table, shared_bias, shift):
    out_row = lax.broadcasted_iota(I32, (LANES, QB), 0)
    n_pairs = (n_heads + 1) // 2
    pair_heads = [[h for h in (2 * p, 2 * p + 1) if h < n_heads] for p in range(n_pairs)]

    def by_head_rows(vals):
        return vals[0] if len(vals) == 1 else jnp.where(out_row < HEAD_DIM, vals[0], vals[1])

    def score(c, maxes, diag):
        start = pl.multiple_of(c * KC, KC)
        shared = None if shared_bias is None else shared_bias(c, diag)
        new = list(maxes)
        for p in range(n_pairs):
            kc = k_ref[pl.ds(start, KC), p * LANES:(p + 1) * LANES]
            for head in pair_heads[p]:
                s = lax.dot_general(kc, qh_ref[head], NT, preferred_element_type=F32) + table(head, diag)
                if shared is not None:
                    s = s + shared
                s_ref[head, c] = s
                new[head] = jnp.maximum(maxes[head], _fold8(s, jnp.max) + shift(c, head))
        return tuple(new)

    maxes = score(qi, tuple(jnp.full((8, QB), NEG_BIG, F32) for _ in range(n_heads)), True)
    maxes = _fori_by_two(qi, lambda c, mx: score(c, mx, False), maxes)
    m = [jnp.max(mx, axis=0, keepdims=True) for mx in maxes]

    acc_ref[...] = jnp.zeros_like(acc_ref)

    def attend(c, sums):
        start = pl.multiple_of(c * KC, KC)
        new = list(sums)
        for p in range(n_pairs):
            vc = v_ref[pl.ds(start, KC), p * LANES:(p + 1) * LANES]
            pv = None
            for head in pair_heads[p]:
                pr = jnp.exp2(s_ref[head, c] + (shift(c, head) - m[head]))
                new[head] = sums[head] + _fold8(pr, jnp.sum)
                vh = vc if len(pair_heads[p]) == 1 else jnp.where(
                    _half_mask(vc.shape, 1, head % 2), vc, jnp.zeros_like(vc))
                part = lax.dot_general(vh, pr.astype(BF16), TN, preferred_element_type=F32)
                pv = part if pv is None else pv + part
            acc_ref[p] += pv
        return tuple(new)

    sums = _fori_by_two(qi + 1, attend, tuple(jnp.zeros((8, QB), F32) for _ in range(n_heads)))
    for p in range(n_pairs):
        denom = by_head_rows([jnp.sum(sums[h], axis=0, keepdims=True) for h in pair_heads[p]])
        o_ref[p * LANES:(p + 1) * LANES, :] = (acc_ref[p] / denom).astype(BF16)


def _alibi_tables(table_ref, slopes, causal):
    row = lax.broadcasted_iota(I32, (KC, QB), 0)
    lane = lax.broadcasted_iota(I32, (KC, QB), 1)
    key_minus_query = (row - lane).astype(F32)
    for head, slope in enumerate(slopes):
        table_ref[head, 0] = slope * key_minus_query
        if causal:
            table_ref[head, 1] = jnp.where(row <= lane, slope * key_minus_query, NEG_BIG)


def _moba_kernel(q_ref, k_ref, v_ref, o_ref, kmean_ref, sel_ref, qh_ref, table_ref, s_ref, acc_ref):
    qi = pl.program_id(1)
    n_blocks = kmean_ref.shape[0]

    @pl.when(qi == 0)
    def _():
        for n in range(n_blocks):
            kb = k_ref[n * MOBA_BLOCK:(n + 1) * MOBA_BLOCK, :].astype(F32)
            kmean_ref[n:n + 1, :] = jnp.mean(kb, axis=0, keepdims=True)
        _alibi_tables(table_ref, SLOPES_MOBA, causal=True)

    blk = lax.broadcasted_iota(I32, (n_blocks, QB), 0)

    _masked_heads(q_ref, qh_ref, H_MOBA)

    for head in range(H_MOBA):
        cols = slice((head // 2) * LANES, (head // 2 + 1) * LANES)
        gate = lax.dot_general(kmean_ref[:, cols], qh_ref[head].astype(F32), NT,
                               preferred_element_type=F32, precision=lax.Precision.HIGHEST)
        rank = jnp.zeros((n_blocks, QB), I32)
        for mb in range(n_blocks):
            gm = gate[mb:mb + 1, :]
            beats = (gm > gate) | ((gm == gate) & (mb < blk))
            rank = rank + jnp.where(beats, (mb < qi).astype(I32), 0)
        selected = ((rank < MOBA_TOPK) & (blk < qi)) | (blk == qi)
        sel_ref[head] = jnp.where(selected, 0.0, NEG_BIG)

    def shift(c, head):
        return sel_ref[head, pl.ds(c, 1), :] + SLOPES_MOBA[head] * ((c - qi) * KC).astype(F32)

    _softmax_heads(qh_ref, k_ref, v_ref, s_ref, acc_ref, o_ref, qi, H_MOBA,
                   table=lambda head, diag: table_ref[head, 1 if diag else 0],
                   shared_bias=None, shift=shift)


def _moba_attention(proj3):
    bsz, seq, _ = proj3.shape
    n_blocks = seq // MOBA_BLOCK
    return pl.pallas_call(
        _moba_kernel,
        grid=(bsz, seq // QB),
        in_specs=[
            pl.BlockSpec((None, QB, PAIR_W), lambda b, i: (b, i, 3)),
            pl.BlockSpec((None, seq, PAIR_W), lambda b, i: (b, 0, 4)),
            pl.BlockSpec((None, seq, PAIR_W), lambda b, i: (b, 0, 5)),
        ],
        out_specs=pl.BlockSpec((None, PAIR_W, QB), lambda b, i: (b, 0, i)),
        out_shape=jax.ShapeDtypeStruct((bsz, PAIR_W, seq), BF16),
        scratch_shapes=[pltpu.VMEM((n_blocks, PAIR_W), F32),
                        pltpu.VMEM((H_MOBA, n_blocks, QB), F32),
                        pltpu.VMEM((H_MOBA, QB, LANES), BF16),
                        pltpu.VMEM((H_MOBA, 2, KC, QB), F32),
                        pltpu.VMEM((H_MOBA, seq // KC, KC, QB), F32),
                        pltpu.VMEM((PAIR_W // LANES, LANES, QB), F32)],
        compiler_params=pltpu.CompilerParams(
            dimension_semantics=("parallel", "arbitrary"),
            vmem_limit_bytes=48 * 1024 * 1024),
        name="moba_attn",
    )(proj3, proj3, proj3)


def _dsa_kernel(q_ref, k_ref, v_ref, qx_ref, kxq_ref, kx_ref, o_ref,
                kx2_ref, score_ref, coarse_ref, tau_ref, cidx_ref, qh_ref, table_ref, s_ref, acc_ref, *, top):
    qi = pl.program_id(1)
    seq = k_ref.shape[0]

    @pl.when(qi == 0)
    def _():
        kx = kx_ref[...].astype(F32)
        dup = jnp.where(_half_mask(kx.shape, 1, 0), kx, pltpu.roll(kx, HEAD_DIM, 1))
        kx2_ref[...] = dup.astype(BF16)
        _alibi_tables(table_ref, SLOPES_DSA, causal=False)

    row = lax.broadcasted_iota(I32, (KC, QB), 0)
    lane = lax.broadcasted_iota(I32, (KC, QB), 1)
    causal_diag = row <= lane

    pick = (lax.broadcasted_iota(I32, (IDX_HEADS, LANES), 1)
            == lax.broadcasted_iota(I32, (IDX_HEADS, LANES), 0) + IDX_DIM).astype(BF16)
    w_t = lax.dot_general(pick, kxq_ref[...], NT, preferred_element_type=F32)

    def score_rows(c, size, diag):
        start = pl.multiple_of(c * KC, KC)
        kc = kx2_ref[pl.ds(start, size), :]
        sc = jnp.zeros((size, QB), F32)
        for pp in range(IDX_HEADS // 2):
            qp = qx_ref[:, pp * LANES:(pp + 1) * LANES]
            for hh in range(2):
                h = 2 * pp + hh
                qh = jnp.where(_half_mask(qp.shape, 1, hh), qp, jnp.zeros_like(qp))
                lg = lax.dot_general(kc, qh, NT, preferred_element_type=F32)
                sc = sc + jnp.maximum(lg, 0.0) * w_t[h:h + 1, :]
        if diag:
            sc = jnp.where(causal_diag, sc, NEG_BIG)
        score_ref[pl.ds(start, size), :] = sc
        coarse_ref[pl.ds(start, size), :] = sc.astype(BF16)

    score_rows(qi, KC, True)

    def _score_two(i, carry):
        score_rows(2 * i, 2 * KC, False)
        return carry

    def _score_one(c, carry):
        score_rows(c, KC, False)
        return carry

    lax.fori_loop(0, qi // 2, _score_two, 0)
    lax.fori_loop(2 * (qi // 2), qi, _score_one, 0)

    def count(n, pred):
        accs = [jnp.zeros((8, QB), I32) for _ in range(4)]
        for c in range(n):
            hit = jnp.where(pred(score_ref[c * KC:(c + 1) * KC, :], c), 1, 0).reshape(KC // 8, 8, QB)
            for r in range(KC // 8):
                accs[r % 4] = accs[r % 4] + hit[r]
        return jnp.sum((accs[0] + accs[1]) + (accs[2] + accs[3]), axis=0, keepdims=True)

    def ordered_to_float(key):
        return pltpu.bitcast(jnp.where(key < 0, key ^ jnp.int32(0x7FFFFFFF), key), F32)

    def count_coarse(n, cand):
        accs = [jnp.zeros((BF16_ROWS, QB), jnp.int16) for _ in range(4)]
        for c in range(n):
            tile = coarse_ref[c * KC:(c + 1) * KC, :].reshape(KC // BF16_ROWS, BF16_ROWS, QB)
            hit = jnp.where(tile >= cand, jnp.int16(1), jnp.int16(0))
            for r in range(KC // BF16_ROWS):
                accs[r % 4] = accs[r % 4] + hit[r]
        total = (accs[0] + accs[1]) + (accs[2] + accs[3])
        return jnp.sum(total.astype(I32), axis=0, keepdims=True)

    def find_threshold(n):
        def coarse_step(i, k):
            cand_k = k + lax.shift_left(jnp.int32(1), 15 - i)
            pattern = jnp.where(cand_k < 0, cand_k ^ jnp.int32(0x7FFF), cand_k)
            cand = pltpu.bitcast(lax.shift_left(pattern, 16), F32).astype(BF16)
            cnt = count_coarse(n, jnp.broadcast_to(cand, (BF16_ROWS, QB))[None])
            return jnp.where(cnt >= top, cand_k, k)

        k16 = lax.fori_loop(0, 16, coarse_step, jnp.full((1, QB), -(2 ** 15), I32))

        def fine_step(i, state):
            key, cnt_key = state
            cand_key = key + lax.shift_left(jnp.int32(1), FINE_BITS - 1 - i)
            cand = ordered_to_float(cand_key)
            cnt = count(n, lambda s, c: s >= cand)
            ok = cnt >= top
            return jnp.where(ok, cand_key, key), jnp.where(ok, cnt, cnt_key)

        start = (lax.shift_left(k16 - 1, 16), jnp.full((1, QB), n * KC, I32))
        key, at_least = lax.fori_loop(0, FINE_BITS, fine_step, start)
        tau = ordered_to_float(key)
        tau_ref[...] = jnp.broadcast_to(tau, tau_ref.shape)

        @pl.when(jnp.max(at_least) > top)
        def _():
            need = top - count(n, lambda s, c: s > tau)

            def idx_step(i, x):
                cand = x + lax.shift_left(jnp.int32(1), 11 - i)
                cnt = count(n, lambda s, c: (s == tau) & ((c * KC + row) < cand))
                return jnp.where(cnt < need, cand, x)
            x = lax.fori_loop(0, 12, idx_step, jnp.zeros((1, QB), I32))
            cidx_ref[...] = jnp.broadcast_to(x, cidx_ref.shape)

    tau_ref[...] = jnp.full(tau_ref.shape, -jnp.inf, F32)
    cidx_ref[...] = jnp.full(cidx_ref.shape, seq, I32)
    for n in range(2, seq // KC + 1):
        pl.when(qi == n - 1)(functools.partial(find_threshold, n))

    tau = tau_ref[0:1, :]
    cidx = cidx_ref[0:1, :]

    def keep_bias(c, diag):
        scores = score_ref[pl.ds(pl.multiple_of(c * KC, KC), KC), :]
        keep = (scores > tau) | ((scores == tau) & ((c * KC + row) <= cidx))
        if diag:
            keep = keep & causal_diag
        return jnp.where(keep, 0.0, NEG_BIG)

    _masked_heads(q_ref, qh_ref, H_DSA)

    _softmax_heads(qh_ref, k_ref, v_ref, s_ref, acc_ref, o_ref, qi, H_DSA,
                   table=lambda head, diag: table_ref[head, 0],
                   shared_bias=keep_bias,
                   shift=lambda c, head: SLOPES_DSA[head] * ((c - qi) * KC).astype(F32))


def _dsa_attention(proj3):
    bsz, seq, _ = proj3.shape
    top = min(DSA_TOPK_MAX, seq // 4)
    assert top == QB, "the first query block must keep every admissible key"
    return pl.pallas_call(
        functools.partial(_dsa_kernel, top=top),
        grid=(bsz, seq // QB),
        in_specs=[
            pl.BlockSpec((None, QB, PAIR_W), lambda b, i: (b, i, 6)),
            pl.BlockSpec((None, seq, PAIR_W), lambda b, i: (b, 0, 7)),
            pl.BlockSpec((None, seq, PAIR_W), lambda b, i: (b, 0, 8)),
            pl.BlockSpec((None, QB, IDX_HEADS * IDX_DIM), lambda b, i: (b, i, COL_QX // (IDX_HEADS * IDX_DIM))),
            pl.BlockSpec((None, QB, LANES), lambda b, i: (b, i, COL_KX // LANES)),
            pl.BlockSpec((None, seq, LANES), lambda b, i: (b, 0, COL_KX // LANES)),
        ],
        out_specs=pl.BlockSpec((None, PAIR_W, QB), lambda b, i: (b, 0, i)),
        out_shape=jax.ShapeDtypeStruct((bsz, PAIR_W, seq), BF16),
        scratch_shapes=[pltpu.VMEM((seq, LANES), BF16),
                        pltpu.VMEM((seq, QB), F32),
                        pltpu.VMEM((seq, QB), BF16),
                        pltpu.VMEM((8, QB), F32),
                        pltpu.VMEM((8, QB), I32),
                        pltpu.VMEM((H_DSA, QB, LANES), BF16),
                        pltpu.VMEM((H_DSA, 1, KC, QB), F32),
                        pltpu.VMEM((H_DSA, seq // KC, KC, QB), F32),
                        pltpu.VMEM((PAIR_W // LANES, LANES, QB), F32)],
        compiler_params=pltpu.CompilerParams(
            dimension_semantics=("parallel", "arbitrary"),
            vmem_limit_bytes=48 * 1024 * 1024),
        name="dsa_attn",
    )(proj3, proj3, proj3, proj3, proj3, proj3)


def _merge_kernel(osb_ref, omb_ref, ods_ref, gsb_ref, gmb_ref, gds_ref, x_ref, mod_ref, gpost_ref,
                  wsb_ref, wmb_ref, wds_ref, wo_ref, out_ref):
    def branch(o_ref, w_ref, g_ref):
        y = lax.dot_general(o_ref[...], w_ref[...], TN, preferred_element_type=F32)
        return g_ref[...].astype(F32) * y

    merged = branch(osb_ref, wsb_ref, gsb_ref) + branch(omb_ref, wmb_ref, gmb_ref) + branch(ods_ref, wds_ref, gds_ref)
    y = jnp.dot(merged.astype(BF16), wo_ref[...], preferred_element_type=F32)
    out_ref[...] = x_ref[...] + mod_ref[2:3, :] * (_rms(y) * gpost_ref[...])


def _merge(o_sb, o_mb, o_ds, proj, x2d, mod, g_post, w_sb, w_mb, w_ds, w_o, layer, seq):
    t, d = x2d.shape
    tm = 1024
    per_b = seq // tm
    gate_blk = COL_GATE // d
    o_spec = pl.BlockSpec((None, PAIR_W, tm), lambda i: (i // per_b, 0, i % per_b))
    w_spec = pl.BlockSpec((None, PAIR_W, d), lambda i: (layer, 0, 0))
    return pl.pallas_call(
        _merge_kernel,
        grid=(t // tm,),
        in_specs=[
            o_spec, o_spec, o_spec,
            pl.BlockSpec((tm, d), lambda i: (i, gate_blk)),
            pl.BlockSpec((tm, d), lambda i: (i, gate_blk + 1)),
            pl.BlockSpec((tm, d), lambda i: (i, gate_blk + 2)),
            pl.BlockSpec((tm, d), lambda i: (i, 0)),
            pl.BlockSpec((None, None, 6, d), lambda i: (layer, i // per_b, 0, 0)),
            pl.BlockSpec((None, 1, d), lambda i: (layer, 0, 0)),
            w_spec, w_spec, w_spec,
            pl.BlockSpec((None, d, d), lambda i: (layer, 0, 0)),
        ],
        out_specs=pl.BlockSpec((tm, d), lambda i: (i, 0)),
        out_shape=jax.ShapeDtypeStruct((t, d), F32),
        compiler_params=pltpu.CompilerParams(
            dimension_semantics=("parallel",),
            vmem_limit_bytes=48 * 1024 * 1024),
        name="merge_out",
    )(o_sb, o_mb, o_ds, proj, proj, proj, x2d, mod, g_post, w_sb, w_mb, w_ds, w_o)


def _ffn_kernel(x_ref, mod_ref, gpre_ref, gpost_ref, wg_ref, wu_ref, wd_ref, out_ref, h_ref, acc_ref):
    xh = _rms(x_ref[...])
    h_ref[...] = ((xh * gpre_ref[...]) * (1.0 + mod_ref[4:5, :]) + mod_ref[3:4, :]).astype(BF16)
    for f in range(0, D_FF, FFN_TF):
        h = h_ref[...]
        gate = jnp.dot(h, wg_ref[:, f:f + FFN_TF], preferred_element_type=F32)
        up = jnp.dot(h, wu_ref[:, f:f + FFN_TF], preferred_element_type=F32)
        act = ((gate * jax.nn.sigmoid(gate)) * up).astype(BF16)
        part = jnp.dot(act, wd_ref[f:f + FFN_TF, :], preferred_element_type=F32)
        if f == 0:
            acc_ref[...] = part
        else:
            acc_ref[...] += part
    y = acc_ref[...]
    out_ref[...] = x_ref[...] + mod_ref[5:6, :] * (_rms(y) * gpost_ref[...])


def _ffn(x2d, mod, g_pre, g_post, w_up, w_down, layer, seq):
    t, d = x2d.shape
    tm = 512
    per_b = seq // tm
    return pl.pallas_call(
        _ffn_kernel,
        grid=(t // tm,),
        in_specs=[
            pl.BlockSpec((tm, d), lambda i: (i, 0)),
            pl.BlockSpec((None, None, 6, d), lambda i: (layer, i // per_b, 0, 0)),
            pl.BlockSpec((None, 1, d), lambda i: (layer, 0, 0)),
            pl.BlockSpec((None, 1, d), lambda i: (layer, 0, 0)),
            pl.BlockSpec((None, d, D_FF), lambda i: (layer, 0, 0)),
            pl.BlockSpec((None, d, D_FF), lambda i: (layer, 0, 1)),
            pl.BlockSpec((None, D_FF, d), lambda i: (layer, 0, 0)),
        ],
        out_specs=pl.BlockSpec((tm, d), lambda i: (i, 0)),
        out_shape=jax.ShapeDtypeStruct((t, d), F32),
        scratch_shapes=[pltpu.VMEM((tm, d), BF16), pltpu.VMEM((tm, d), F32)],
        compiler_params=pltpu.CompilerParams(
            dimension_semantics=("parallel",),
            vmem_limit_bytes=56 * 1024 * 1024),
        name="ffn",
    )(x2d, mod, g_pre, g_post, w_up, w_up, w_down)


def _pad_last(w, n):
    return jnp.pad(w, [(0, 0)] * (w.ndim - 1) + [(0, n - w.shape[-1])])


def _pad_rows(w, n):
    return jnp.pad(w, ((0, 0), (0, n - w.shape[1]), (0, 0)))


def _layout_w_in(w_in, b_gate):
    sizes = [W_SB] * 3 + [W_MOBA] * 3 + [W_DSA] * 3 + [IDX_HEADS * IDX_DIM, IDX_DIM, IDX_HEADS, N_BRANCH * D_MODEL]
    points = [int(v) for v in np.cumsum(sizes)[:-1]]
    scale = HEAD_DIM ** -0.5 * LOG2E
    col_scale = np.ones((sum(sizes),), np.float32)
    for q_seg in (0, 3, 6):
        col_scale[sum(sizes[:q_seg]):sum(sizes[:q_seg + 1])] = scale
    w16 = (w_in * col_scale).astype(BF16)
    (q_sb, k_sb, v_sb, q_mb, k_mb, v_mb, q_ds, k_ds, v_ds, q_ix, k_ix, w_ix, gates) = jnp.split(w16, points, axis=-1)
    cols = [q_sb, k_sb, v_sb,
            _pad_last(q_mb, PAIR_W), _pad_last(k_mb, PAIR_W), _pad_last(v_mb, PAIR_W),
            _pad_last(q_ds, PAIR_W), _pad_last(k_ds, PAIR_W), _pad_last(v_ds, PAIR_W),
            _pad_last(jnp.concatenate([k_ix, w_ix], axis=-1), LANES), q_ix, gates]
    w_all = jnp.concatenate(cols, axis=-1)
    bias = jnp.concatenate([jnp.zeros((b_gate.shape[0], COL_GATE), F32), b_gate], axis=-1)[:, None, :]
    return w_all, bias


def kernel(x, c, w_ada, b_ada, g_pre_mix, g_post_mix, w_in, b_gate, w_proj_sb, w_proj_moba,
           w_proj_dsa, w_o, g_pre_ffn, g_post_ffn, w_up, w_down):
    bsz, seq, d = x.shape
    depth = w_ada.shape[0]
    assert d == D_MODEL and seq % QB == 0 and QB == MOBA_BLOCK and KC == QB
    mod = _ada(c, w_ada, b_ada).reshape(depth, bsz, 6, d)
    w_all, bias = _layout_w_in(w_in, b_gate)
    assert w_all.shape[-1] == N_PROJ
    w_sb = w_proj_sb.astype(BF16)
    w_mb = _pad_rows(w_proj_moba, PAIR_W).astype(BF16)
    w_ds = _pad_rows(w_proj_dsa, PAIR_W).astype(BF16)
    w_o, w_up, w_down = w_o.astype(BF16), w_up.astype(BF16), w_down.astype(BF16)
    rows = lambda g: g[:, None, :]
    x2d = x.reshape(bsz * seq, d)
    for layer in range(depth):
        proj = _inproj(x2d, mod, rows(g_pre_mix), w_all, bias, layer, seq)
        proj3 = proj.reshape(bsz, seq, N_PROJ)
        o_sb = _sb_attention(proj3)
        o_mb = _moba_attention(proj3)
        o_ds = _dsa_attention(proj3)
        x2d = _merge(o_sb, o_mb, o_ds, proj, x2d, mod, rows(g_post_mix), w_sb, w_mb, w_ds, w_o, layer, seq)
        x2d = _ffn(x2d, mod, rows(g_pre_ffn), rows(g_post_ffn), w_up, w_down, layer, seq)
    return x2d.reshape(bsz, seq, d)
```

```python
import functools

import numpy as np
import jax
import jax.numpy as jnp
from jax import lax
from jax.experimental import pallas as pl
from jax.experimental.pallas import tpu as pltpu

F32 = jnp.float32
BF16 = jnp.bfloat16
I32 = jnp.int32

D_MODEL = 1024
HEAD_DIM = 64
H_SB, H_MOBA, H_DSA = 6, 5, 5
W_SB, W_MOBA, W_DSA = H_SB * HEAD_DIM, H_MOBA * HEAD_DIM, H_DSA * HEAD_DIM
MOBA_BLOCK = 256
MOBA_TOPK = 3
DSA_TOPK_MAX = 256
IDX_HEADS = 8
IDX_DIM = 64
D_FF = 2816
N_BRANCH = 3
RMS_EPS = 1e-6
NEG_BIG = -1e30
ALIBI_HEADS = H_MOBA + H_DSA

LANES = 128
PAIR_W = 3 * LANES
QB = 256
KC = 256
INT_MIN = -(2 ** 31)
BF16_ROWS = 16
FINE_BITS = 18

COL_KX = 9 * PAIR_W
COL_QX = COL_KX + LANES
COL_GATE = COL_QX + IDX_HEADS * IDX_DIM
N_PROJ = COL_GATE + N_BRANCH * D_MODEL
PROJ_TN = 1024
FFN_TF = 256

NT = (((1,), (1,)), ((), ()))
TN = (((0,), (0,)), ((), ()))

LOG2E = 1.4426950408889634
_ALIBI = [float(2.0 ** (-8.0 * h / ALIBI_HEADS)) * LOG2E for h in range(1, ALIBI_HEADS + 1)]
SLOPES_MOBA = _ALIBI[0::2]
SLOPES_DSA = _ALIBI[1::2]


def _rms(x):
    return x * lax.rsqrt(jnp.mean(x * x, axis=-1, keepdims=True) + RMS_EPS)


def _fori_by_two(n, body, init):
    def two(i, state):
        return body(2 * i + 1, body(2 * i, state))
    state = lax.fori_loop(0, n // 2, two, init)
    return lax.fori_loop(2 * (n // 2), n, body, state)


def _half_mask(shape, lane_axis, hh):
    lane = lax.broadcasted_iota(I32, shape, lane_axis)
    return (lane < HEAD_DIM) if hh == 0 else (lane >= HEAD_DIM)


def _ada_kernel(c_ref, w_ref, b_ref, o_ref):
    c = c_ref[...]
    ca = c * jax.nn.sigmoid(c)
    o_ref[...] = jnp.dot(ca, w_ref[...], preferred_element_type=F32) + b_ref[...]


def _ada(c, w_ada, b_ada):
    depth, d, n = w_ada.shape
    bsz = c.shape[0]
    tn = 512
    return pl.pallas_call(
        _ada_kernel,
        grid=(depth, n // tn),
        in_specs=[
            pl.BlockSpec((bsz, d), lambda l, j: (0, 0)),
            pl.BlockSpec((None, d, tn), lambda l, j: (l, 0, j)),
            pl.BlockSpec((None, 1, tn), lambda l, j: (l, 0, j)),
        ],
        out_specs=pl.BlockSpec((None, bsz, tn), lambda l, j: (l, 0, j)),
        out_shape=jax.ShapeDtypeStruct((depth, bsz, n), F32),
        name="ada_mod",
    )(c, w_ada, b_ada.reshape(depth, 1, n))


def _inproj_kernel(x_ref, mod_ref, g_ref, w_ref, b_ref, o_ref):
    xh = _rms(x_ref[...])
    h = ((xh * g_ref[...]) * (1.0 + mod_ref[1:2, :]) + mod_ref[0:1, :]).astype(BF16)
    for n0 in range(0, N_PROJ, PROJ_TN):
        cols = slice(n0, n0 + PROJ_TN)
        acc = jnp.dot(h, w_ref[:, cols], preferred_element_type=F32) + b_ref[:, cols]
        if n0 >= COL_GATE:
            acc = jax.nn.sigmoid(acc)
        o_ref[:, cols] = acc.astype(BF16)


def _inproj(x2d, mod, g_pre, w_all, bias, layer, seq):
    t, d = x2d.shape
    tm = 512
    n = w_all.shape[-1]
    per_b = seq // tm
    return pl.pallas_call(
        _inproj_kernel,
        grid=(t // tm,),
        in_specs=[
            pl.BlockSpec((tm, d), lambda i: (i, 0)),
            pl.BlockSpec((None, None, 6, d), lambda i: (layer, i // per_b, 0, 0)),
            pl.BlockSpec((None, 1, d), lambda i: (layer, 0, 0)),
            pl.BlockSpec((None, d, n), lambda i: (layer, 0, 0)),
            pl.BlockSpec((None, 1, n), lambda i: (layer, 0, 0)),
        ],
        out_specs=pl.BlockSpec((tm, n), lambda i: (i, 0)),
        out_shape=jax.ShapeDtypeStruct((t, n), BF16),
        compiler_params=pltpu.CompilerParams(
            dimension_semantics=("parallel",),
            vmem_limit_bytes=56 * 1024 * 1024),
        name="in_proj",
    )(x2d, mod, g_pre, w_all, bias)


def _masked_heads(ref, dst_ref, n_heads):
    for head in range(n_heads):
        grp = ref[:, (head // 2) * LANES:(head // 2 + 1) * LANES]
        dst_ref[head] = jnp.where(_half_mask(grp.shape, 1, head % 2), grp, jnp.zeros_like(grp))


def _sb_kernel(q_ref, k_ref, v_ref, o_ref, qh_ref, lb_ref, l1m_ref, carry_ref, acc_ref):
    qi = pl.program_id(1)
    row = lax.broadcasted_iota(I32, (KC, QB), 0)
    lane = lax.broadcasted_iota(I32, (KC, QB), 1)
    past_diag = row < lane
    upper = (lax.broadcasted_iota(I32, (KC, KC), 1) > lax.broadcasted_iota(I32, (KC, KC), 0)).astype(BF16)

    _masked_heads(q_ref, qh_ref, H_SB)

    def terms(c, carries, diag):
        start = pl.multiple_of(c * KC, KC)
        new = list(carries)
        for p in range(H_SB // 2):
            kc = k_ref[pl.ds(start, KC), p * LANES:(p + 1) * LANES]
            for head in (2 * p, 2 * p + 1):
                z = lax.dot_general(kc, qh_ref[head], NT, preferred_element_type=F32)
                log_beta = jnp.minimum(z, 0.0) - jnp.log2(1.0 + jnp.exp2(-jnp.abs(z)))
                log_1m = log_beta - z
                if diag:
                    log_1m = jnp.where(past_diag, log_1m, 0.0)
                    log_beta = jnp.where(past_diag, log_beta, NEG_BIG)
                lb_ref[head, c] = log_beta
                l1m_ref[head, c] = log_1m.astype(BF16)
                carry_ref[head, c] = carries[head]
                new[head] = carries[head] + _fold8(log_1m, jnp.sum)
        return tuple(new)

    def attend(c):
        start = pl.multiple_of(c * KC, KC)
        for p in range(H_SB // 2):
            vc = v_ref[pl.ds(start, KC), p * LANES:(p + 1) * LANES]
            pv = None
            for head in (2 * p, 2 * p + 1):
                between = jnp.dot(upper, l1m_ref[head, c], preferred_element_type=F32)
                carry = jnp.sum(carry_ref[head, c], axis=0, keepdims=True)
                a = jnp.exp2(lb_ref[head, c] + between + carry)
                vh = jnp.where(_half_mask(vc.shape, 1, head % 2), vc, jnp.zeros_like(vc))
                part = lax.dot_general(vh, a.astype(BF16), TN, preferred_element_type=F32)
                pv = part if pv is None else pv + part
            acc_ref[p] += pv

    def pipelined(i, carries):
        c = qi - 1 - i
        attend(c + 1)
        return terms(c, carries, False)

    acc_ref[...] = jnp.zeros_like(acc_ref)
    carries = terms(qi, tuple(jnp.zeros((8, QB), F32) for _ in range(H_SB)), True)
    _fori_by_two(qi, pipelined, carries)
    attend(0)
    for p in range(H_SB // 2):
        o_ref[p * LANES:(p + 1) * LANES, :] = acc_ref[p].astype(BF16)


def _sb_attention(proj3):
    bsz, seq, _ = proj3.shape
    return pl.pallas_call(
        _sb_kernel,
        grid=(bsz, seq // QB),
        in_specs=[
            pl.BlockSpec((None, QB, PAIR_W), lambda b, i: (b, i, 0)),
            pl.BlockSpec((None, seq, PAIR_W), lambda b, i: (b, 0, 1)),
            pl.BlockSpec((None, seq, PAIR_W), lambda b, i: (b, 0, 2)),
        ],
        out_specs=pl.BlockSpec((None, PAIR_W, QB), lambda b, i: (b, 0, i)),
        out_shape=jax.ShapeDtypeStruct((bsz, PAIR_W, seq), BF16),
        scratch_shapes=[pltpu.VMEM((H_SB, QB, LANES), BF16),
                        pltpu.VMEM((H_SB, seq // KC, KC, QB), F32),
                        pltpu.VMEM((H_SB, seq // KC, KC, QB), BF16),
                        pltpu.VMEM((H_SB, seq // KC, 8, QB), F32),
                        pltpu.VMEM((H_SB // 2, LANES, QB), F32)],
        compiler_params=pltpu.CompilerParams(
            dimension_semantics=("parallel", "arbitrary"),
            vmem_limit_bytes=48 * 1024 * 1024),
        name="sb_attn",
    )(proj3, proj3, proj3)


def _fold8(x, op):
    return op(x.reshape(KC // 8, 8, QB), axis=0)


def _softmax_heads(qh_ref, k_ref, v_ref, s_ref, acc_ref, o_ref, qi, n_heads, table, shared_bias, shift):
    out_row = lax.broadcasted_iota(I32, (LANES, QB), 0)
    n_pairs = (n_heads + 1) // 2
    pair_heads = [[h for h in (2 * p, 2 * p + 1) if h < n_heads] for p in range(n_pairs)]

    def by_head_rows(vals):
        return vals[0] if len(vals) == 1 else jnp.where(out_row < HEAD_DIM, vals[0], vals[1])

    def score(c, maxes, diag):
        start = pl.multiple_of(c * KC, KC)
        shared = None if shared_bias is None else shared_bias(c, diag)
        new = list(maxes)
        for p in range(n_pairs):
            kc = k_ref[pl.ds(start, KC), p * LANES:(p + 1) * LANES]
            for head in pair_heads[p]:
                s = lax.dot_general(kc, qh_ref[head], NT, preferred_element_type=F32) + table(head, diag)
                if shared is not None:
                    s = s + shared
                s_ref[head, c] = s
                new[head] = jnp.maximum(maxes[head], _fold8(s, jnp.max) + shift(c, head))
        return tuple(new)

    maxes = score(qi, tuple(jnp.full((8, QB), NEG_BIG, F32) for _ in range(n_heads)), True)
    maxes = _fori_by_two(qi, lambda c, mx: score(c, mx, False), maxes)
    m = [jnp.max(mx, axis=0, keepdims=True) for mx in maxes]

    acc_ref[...] = jnp.zeros_like(acc_ref)

    def attend(c, sums):
        start = pl.multiple_of(c * KC, KC)
        new = list(sums)
        for p in range(n_pairs):
            vc = v_ref[pl.ds(start, KC), p * LANES:(p + 1) * LANES]
            pv = None
            for head in pair_heads[p]:
                pr = jnp.exp2(s_ref[head, c] + (shift(c, head) - m[head]))
                new[head] = sums[head] + _fold8(pr, jnp.sum)
                vh = vc if len(pair_heads[p]) == 1 else jnp.where(
                    _half_mask(vc.shape, 1, head % 2), vc, jnp.zeros_like(vc))
                part = lax.dot_general(vh, pr.astype(BF16), TN, preferred_element_type=F32)
                pv = part if pv is None else pv + part
            acc_ref[p] += pv
        return tuple(new)

    sums = _fori_by_two(qi + 1, attend, tuple(jnp.zeros((8, QB), F32) for _ in range(n_heads)))
    for p in range(n_pairs):
        denom = by_head_rows([jnp.sum(sums[h], axis=0, keepdims=True) for h in pair_heads[p]])
        o_ref[p * LANES:(p + 1) * LANES, :] = (acc_ref[p] / denom).astype(BF16)


def _alibi_tables(table_ref, slopes, causal):
    row = lax.broadcasted_iota(I32, (KC, QB), 0)
    lane = lax.broadcasted_iota(I32, (KC, QB), 1)
    key_minus_query = (row - lane).astype(F32)
    for head, slope in enumerate(slopes):
        table_ref[head, 0] = slope * key_minus_query
        if causal:
            table_ref[head, 1] = jnp.where(row <= lane, slope * key_minus_query, NEG_BIG)


def _moba_kernel(q_ref, k_ref, v_ref, o_ref, kmean_ref, sel_ref, qh_ref, table_ref, s_ref, acc_ref):
    qi = pl.program_id(1)
    n_blocks = kmean_ref.shape[0]

    @pl.when(qi == 0)
    def _():
        for n in range(n_blocks):
            kb = k_ref[n * MOBA_BLOCK:(n + 1) * MOBA_BLOCK, :].astype(F32)
            kmean_ref[n:n + 1, :] = jnp.mean(kb, axis=0, keepdims=True)
        _alibi_tables(table_ref, SLOPES_MOBA, causal=True)

    blk = lax.broadcasted_iota(I32, (n_blocks, QB), 0)

    _masked_heads(q_ref, qh_ref, H_MOBA)

    for head in range(H_MOBA):
        cols = slice((head // 2) * LANES, (head // 2 + 1) * LANES)
        gate = lax.dot_general(kmean_ref[:, cols], qh_ref[head].astype(F32), NT,
                               preferred_element_type=F32, precision=lax.Precision.HIGHEST)
        rank = jnp.zeros((n_blocks, QB), I32)
        for mb in range(n_blocks):
            gm = gate[mb:mb + 1, :]
            beats = (gm > gate) | ((gm == gate) & (mb < blk))
            rank = rank + jnp.where(beats, (mb < qi).astype(I32), 0)
        selected = ((rank < MOBA_TOPK) & (blk < qi)) | (blk == qi)
        sel_ref[head] = jnp.where(selected, 0.0, NEG_BIG)

    def shift(c, head):
        return sel_ref[head, pl.ds(c, 1), :] + SLOPES_MOBA[head] * ((c - qi) * KC).astype(F32)

    _softmax_heads(qh_ref, k_ref, v_ref, s_ref, acc_ref, o_ref, qi, H_MOBA,
                   table=lambda head, diag: table_ref[head, 1 if diag else 0],
                   shared_bias=None, shift=shift)


def _moba_attention(proj3):
    bsz, seq, _ = proj3.shape
    n_blocks = seq // MOBA_BLOCK
    return pl.pallas_call(
        _moba_kernel,
        grid=(bsz, seq // QB),
        in_specs=[
            pl.BlockSpec((None, QB, PAIR_W), lambda b, i: (b, i, 3)),
            pl.BlockSpec((None, seq, PAIR_W), lambda b, i: (b, 0, 4)),
            pl.BlockSpec((None, seq, PAIR_W), lambda b, i: (b, 0, 5)),
        ],
        out_specs=pl.BlockSpec((None, PAIR_W, QB), lambda b, i: (b, 0, i)),
        out_shape=jax.ShapeDtypeStruct((bsz, PAIR_W, seq), BF16),
        scratch_shapes=[pltpu.VMEM((n_blocks, PAIR_W), F32),
                        pltpu.VMEM((H_MOBA, n_blocks, QB), F32),
                        pltpu.VMEM((H_MOBA, QB, LANES), BF16),
                        pltpu.VMEM((H_MOBA, 2, KC, QB), F32),
                        pltpu.VMEM((H_MOBA, seq // KC, KC, QB), F32),
                        pltpu.VMEM((PAIR_W // LANES, LANES, QB), F32)],
        compiler_params=pltpu.CompilerParams(
            dimension_semantics=("parallel", "arbitrary"),
            vmem_limit_bytes=48 * 1024 * 1024),
        name="moba_attn",
    )(proj3, proj3, proj3)


def _dsa_kernel(q_ref, k_ref, v_ref, qx_ref, kxq_ref, kx_ref, o_ref,
                kx2_ref, score_ref, coarse_ref, tau_ref, cidx_ref, qh_ref, table_ref, s_ref, acc_ref, *, top):
    qi = pl.program_id(1)
    seq = k_ref.shape[0]

    @pl.when(qi == 0)
    def _():
        kx = kx_ref[...].astype(F32)
        dup = jnp.where(_half_mask(kx.shape, 1, 0), kx, pltpu.roll(kx, HEAD_DIM, 1))
        kx2_ref[...] = dup.astype(BF16)
        _alibi_tables(table_ref, SLOPES_DSA, causal=False)

    row = lax.broadcasted_iota(I32, (KC, QB), 0)
    lane = lax.broadcasted_iota(I32, (KC, QB), 1)
    causal_diag = row <= lane

    pick = (lax.broadcasted_iota(I32, (IDX_HEADS, LANES), 1)
            == lax.broadcasted_iota(I32, (IDX_HEADS, LANES), 0) + IDX_DIM).astype(BF16)
    w_t = lax.dot_general(pick, kxq_ref[...], NT, preferred_element_type=F32)

    def score_rows(c, size, diag):
        start = pl.multiple_of(c * KC, KC)
        kc = kx2_ref[pl.ds(start, size), :]
        sc = jnp.zeros((size, QB), F32)
        for pp in range(IDX_HEADS // 2):
            qp = qx_ref[:, pp * LANES:(pp + 1) * LANES]
            for hh in range(2):
                h = 2 * pp + hh
                qh = jnp.where(_half_mask(qp.shape, 1, hh), qp, jnp.zeros_like(qp))
                lg = lax.dot_general(kc, qh, NT, preferred_element_type=F32)
                sc = sc + jnp.maximum(lg, 0.0) * w_t[h:h + 1, :]
        if diag:
            sc = jnp.where(causal_diag, sc, NEG_BIG)
        score_ref[pl.ds(start, size), :] = sc
        coarse_ref[pl.ds(start, size), :] = sc.astype(BF16)

    score_rows(qi, KC, True)

    def _score_two(i, carry):
        score_rows(2 * i, 2 * KC, False)
        return carry

    def _score_one(c, carry):
        score_rows(c, KC, False)
        return carry

    lax.fori_loop(0, qi // 2, _score_two, 0)
    lax.fori_loop(2 * (qi // 2), qi, _score_one, 0)

    def count(n, pred):
        accs = [jnp.zeros((8, QB), I32) for _ in range(4)]
        for c in range(n):
            hit = jnp.where(pred(score_ref[c * KC:(c + 1) * KC, :], c), 1, 0).reshape(KC // 8, 8, QB)
            for r in range(KC // 8):
                accs[r % 4] = accs[r % 4] + hit[r]
        return jnp.sum((accs[0] + accs[1]) + (accs[2] + accs[3]), axis=0, keepdims=True)

    def ordered_to_float(key):
        return pltpu.bitcast(jnp.where(key < 0, key ^ jnp.int32(0x7FFFFFFF), key), F32)

    def count_coarse(n, cand):
        accs = [jnp.zeros((BF16_ROWS, QB), jnp.int16) for _ in range(4)]
        for c in range(n):
            tile = coarse_ref[c * KC:(c + 1) * KC, :].reshape(KC // BF16_ROWS, BF16_ROWS, QB)
            hit = jnp.where(tile >= cand, jnp.int16(1), jnp.int16(0))
            for r in range(KC // BF16_ROWS):
                accs[r % 4] = accs[r % 4] + hit[r]
        total = (accs[0] + accs[1]) + (accs[2] + accs[3])
        return jnp.sum(total.astype(I32), axis=0, keepdims=True)

    def find_threshold(n):
        def coarse_step(i, k):
            cand_k = k + lax.shift_left(jnp.int32(1), 15 - i)
            pattern = jnp.where(cand_k < 0, cand_k ^ jnp.int32(0x7FFF), cand_k)
            cand = pltpu.bitcast(lax.shift_left(pattern, 16), F32).astype(BF16)
            cnt = count_coarse(n, jnp.broadcast_to(cand, (BF16_ROWS, QB))[None])
            return jnp.where(cnt >= top, cand_k, k)

        k16 = lax.fori_loop(0, 16, coarse_step, jnp.full((1, QB), -(2 ** 15), I32))

        def fine_step(i, state):
            key, cnt_key = state
            cand_key = key + lax.shift_left(jnp.int32(1), FINE_BITS - 1 - i)
            cand = ordered_to_float(cand_key)
            cnt = count(n, lambda s, c: s >= cand)
            ok = cnt >= top
            return jnp.where(ok, cand_key, key), jnp.where(ok, cnt, cnt_key)

        start = (lax.shift_left(k16 - 1, 16), jnp.full((1, QB), n * KC, I32))
        key, at_least = lax.fori_loop(0, FINE_BITS, fine_step, start)
        tau = ordered_to_float(key)
        tau_ref[...] = jnp.broadcast_to(tau, tau_ref.shape)

        @pl.when(jnp.max(at_least) > top)
        def _():
            need = top - count(n, lambda s, c: s > tau)

            def idx_step(i, x):
                cand = x + lax.shift_left(jnp.int32(1), 11 - i)
                cnt = count(n, lambda s, c: (s == tau) & ((c * KC + row) < cand))
                return jnp.where(cnt < need, cand, x)
            x = lax.fori_loop(0, 12, idx_step, jnp.zeros((1, QB), I32))
            cidx_ref[...] = jnp.broadcast_to(x, cidx_ref.shape)

    tau_ref[...] = jnp.full(tau_ref.shape, -jnp.inf, F32)
    cidx_ref[...] = jnp.full(cidx_ref.shape, seq, I32)
    for n in range(2, seq // KC + 1):
        pl.when(qi == n - 1)(functools.partial(find_threshold, n))

    tau = tau_ref[0:1, :]
    cidx = cidx_ref[0:1, :]

    def keep_bias(c, diag):
        scores = score_ref[pl.ds(pl.multiple_of(c * KC, KC), KC), :]
        keep = (scores > tau) | ((scores == tau) & ((c * KC + row) <= cidx))
        if diag:
            keep = keep & causal_diag
        return jnp.where(keep, 0.0, NEG_BIG)

    _masked_heads(q_ref, qh_ref, H_DSA)

    _softmax_heads(qh_ref, k_ref, v_ref, s_ref, acc_ref, o_ref, qi, H_DSA,
                   table=lambda head, diag: table_ref[head, 0],
                   shared_bias=keep_bias,
                   shift=lambda c, head: SLOPES_DSA[head] * ((c - qi) * KC).astype(F32))


def _dsa_attention(proj3):
    bsz, seq, _ = proj3.shape
    top = min(DSA_TOPK_MAX, seq // 4)
    assert top == QB, "the first query block must keep every admissible key"
    return pl.pallas_call(
        functools.partial(_dsa_kernel, top=top),
        grid=(bsz, seq // QB),
        in_specs=[
            pl.BlockSpec((None, QB, PAIR_W), lambda b, i: (b, i, 6)),
            pl.BlockSpec((None, seq, PAIR_W), lambda b, i: (b, 0, 7)),
            pl.BlockSpec((None, seq, PAIR_W), lambda b, i: (b, 0, 8)),
            pl.BlockSpec((None, QB, IDX_HEADS * IDX_DIM), lambda b, i: (b, i, COL_QX // (IDX_HEADS * IDX_DIM))),
            pl.BlockSpec((None, QB, LANES), lambda b, i: (b, i, COL_KX // LANES)),
            pl.BlockSpec((None, seq, LANES), lambda b, i: (b, 0, COL_KX // LANES)),
        ],
        out_specs=pl.BlockSpec((None, PAIR_W, QB), lambda b, i: (b, 0, i)),
        out_shape=jax.ShapeDtypeStruct((bsz, PAIR_W, seq), BF16),
        scratch_shapes=[pltpu.VMEM((seq, LANES), BF16),
                        pltpu.VMEM((seq, QB), F32),
                        pltpu.VMEM((seq, QB), BF16),
                        pltpu.VMEM((8, QB), F32),
                        pltpu.VMEM((8, QB), I32),
                        pltpu.VMEM((H_DSA, QB, LANES), BF16),
                        pltpu.VMEM((H_DSA, 1, KC, QB), F32),
                        pltpu.VMEM((H_DSA, seq // KC, KC, QB), F32),
                        pltpu.VMEM((PAIR_W // LANES, LANES, QB), F32)],
        compiler_params=pltpu.CompilerParams(
            dimension_semantics=("parallel", "arbitrary"),
            vmem_limit_bytes=48 * 1024 * 1024),
        name="dsa_attn",
    )(proj3, proj3, proj3, proj3, proj3, proj3)


def _merge_kernel(osb_ref, omb_ref, ods_ref, gsb_ref, gmb_ref, gds_ref, x_ref, mod_ref, gpost_ref,
                  wsb_ref, wmb_ref, wds_ref, wo_ref, out_ref):
    def branch(o_ref, w_ref, g_ref):
        y = lax.dot_general(o_ref[...], w_ref[...], TN, preferred_element_type=F32)
        return g_ref[...].astype(F32) * y

    merged = branch(osb_ref, wsb_ref, gsb_ref) + branch(omb_ref, wmb_ref, gmb_ref) + branch(ods_ref, wds_ref, gds_ref)
    y = jnp.dot(merged.astype(BF16), wo_ref[...], preferred_element_type=F32)
    out_ref[...] = x_ref[...] + mod_ref[2:3, :] * (_rms(y) * gpost_ref[...])


def _merge(o_sb, o_mb, o_ds, proj, x2d, mod, g_post, w_sb, w_mb, w_ds, w_o, layer, seq):
    t, d = x2d.shape
    tm = 1024
    per_b = seq // tm
    gate_blk = COL_GATE // d
    o_spec = pl.BlockSpec((None, PAIR_W, tm), lambda i: (i // per_b, 0, i % per_b))
    w_spec = pl.BlockSpec((None, PAIR_W, d), lambda i: (layer, 0, 0))
    return pl.pallas_call(
        _merge_kernel,
        grid=(t // tm,),
        in_specs=[
            o_spec, o_spec, o_spec,
            pl.BlockSpec((tm, d), lambda i: (i, gate_blk)),
            pl.BlockSpec((tm, d), lambda i: (i, gate_blk + 1)),
            pl.BlockSpec((tm, d), lambda i: (i, gate_blk + 2)),
            pl.BlockSpec((tm, d), lambda i: (i, 0)),
            pl.BlockSpec((None, None, 6, d), lambda i: (layer, i // per_b, 0, 0)),
            pl.BlockSpec((None, 1, d), lambda i: (layer, 0, 0)),
            w_spec, w_spec, w_spec,
            pl.BlockSpec((None, d, d), lambda i: (layer, 0, 0)),
        ],
        out_specs=pl.BlockSpec((tm, d), lambda i: (i, 0)),
        out_shape=jax.ShapeDtypeStruct((t, d), F32),
        compiler_params=pltpu.CompilerParams(
            dimension_semantics=("parallel",),
            vmem_limit_bytes=48 * 1024 * 1024),
        name="merge_out",
    )(o_sb, o_mb, o_ds, proj, proj, proj, x2d, mod, g_post, w_sb, w_mb, w_ds, w_o)


def _ffn_kernel(x_ref, mod_ref, gpre_ref, gpost_ref, wg_ref, wu_ref, wd_ref, out_ref, h_ref, acc_ref):
    xh = _rms(x_ref[...])
    h_ref[...] = ((xh * gpre_ref[...]) * (1.0 + mod_ref[4:5, :]) + mod_ref[3:4, :]).astype(BF16)
    for f in range(0, D_FF, FFN_TF):
        h = h_ref[...]
        gate = jnp.dot(h, wg_ref[:, f:f + FFN_TF], preferred_element_type=F32)
        up = jnp.dot(h, wu_ref[:, f:f + FFN_TF], preferred_element_type=F32)
        act = ((gate * jax.nn.sigmoid(gate)) * up).astype(BF16)
        part = jnp.dot(act, wd_ref[f:f + FFN_TF, :], preferred_element_type=F32)
        if f == 0:
            acc_ref[...] = part
        else:
            acc_ref[...] += part
    y = acc_ref[...]
    out_ref[...] = x_ref[...] + mod_ref[5:6, :] * (_rms(y) * gpost_ref[...])


def _ffn(x2d, mod, g_pre, g_post, w_up, w_down, layer, seq):
    t, d = x2d.shape
    tm = 512
    per_b = seq // tm
    return pl.pallas_call(
        _ffn_kernel,
        grid=(t // tm,),
        in_specs=[
            pl.BlockSpec((tm, d), lambda i: (i, 0)),
            pl.BlockSpec((None, None, 6, d), lambda i: (layer, i // per_b, 0, 0)),
            pl.BlockSpec((None, 1, d), lambda i: (layer, 0, 0)),
            pl.BlockSpec((None, 1, d), lambda i: (layer, 0, 0)),
            pl.BlockSpec((None, d, D_FF), lambda i: (layer, 0, 0)),
            pl.BlockSpec((None, d, D_FF), lambda i: (layer, 0, 1)),
            pl.BlockSpec((None, D_FF, d), lambda i: (layer, 0, 0)),
        ],
        out_specs=pl.BlockSpec((tm, d), lambda i: (i, 0)),
        out_shape=jax.ShapeDtypeStruct((t, d), F32),
        scratch_shapes=[pltpu.VMEM((tm, d), BF16), pltpu.VMEM((tm, d), F32)],
        compiler_params=pltpu.CompilerParams(
            dimension_semantics=("parallel",),
            vmem_limit_bytes=56 * 1024 * 1024),
        name="ffn",
    )(x2d, mod, g_pre, g_post, w_up, w_up, w_down)


def _pad_rows(w, n):
    return jnp.pad(w, ((0, 0), (0, n - w.shape[1]), (0, 0)))


def _w_in_segments():
    sizes = [W_SB] * 3 + [W_MOBA] * 3 + [W_DSA] * 3 + [IDX_HEADS * IDX_DIM, IDX_DIM + IDX_HEADS, N_BRANCH * D_MODEL]
    src = [int(v) for v in np.cumsum([0] + sizes[:-1])]
    q_scale = HEAD_DIM ** -0.5 * LOG2E
    dst = [j * PAIR_W for j in range(9)] + [COL_QX, COL_KX, COL_GATE]
    scales = [q_scale, 1.0, 1.0] * 3 + [1.0, 1.0, 1.0]
    return list(zip(src, sizes, dst, scales))


def _w_layout_kernel(w_ref, o_ref):
    o_ref[...] = jnp.zeros_like(o_ref)
    for src, width, dst, scale in _w_in_segments():
        o_ref[:, dst:dst + width] = (w_ref[:, src:src + width] * scale).astype(BF16)


def _layout_w_in(w_in, b_gate):
    depth, d, d_in = w_in.shape
    rows = 128
    w_all = pl.pallas_call(
        _w_layout_kernel,
        grid=(depth, d // rows),
        in_specs=[pl.BlockSpec((None, rows, d_in), lambda l, i: (l, i, 0))],
        out_specs=pl.BlockSpec((None, rows, N_PROJ), lambda l, i: (l, i, 0)),
        out_shape=jax.ShapeDtypeStruct((depth, d, N_PROJ), BF16),
        name="w_layout",
    )(w_in)
    bias = jnp.concatenate([jnp.zeros((depth, COL_GATE), F32), b_gate], axis=-1)[:, None, :]
    return w_all, bias


def kernel(x, c, w_ada, b_ada, g_pre_mix, g_post_mix, w_in, b_gate, w_proj_sb, w_proj_moba,
           w_proj_dsa, w_o, g_pre_ffn, g_post_ffn, w_up, w_down):
    bsz, seq, d = x.shape
    depth = w_ada.shape[0]
    assert d == D_MODEL and seq % QB == 0 and QB == MOBA_BLOCK and KC == QB
    mod = _ada(c, w_ada, b_ada).reshape(depth, bsz, 6, d)
    w_all, bias = _layout_w_in(w_in, b_gate)
    assert w_all.shape[-1] == N_PROJ
    w_sb = w_proj_sb.astype(BF16)
    w_mb = _pad_rows(w_proj_moba, PAIR_W).astype(BF16)
    w_ds = _pad_rows(w_proj_dsa, PAIR_W).astype(BF16)
    w_o, w_up, w_down = w_o.astype(BF16), w_up.astype(BF16), w_down.astype(BF16)
    rows = lambda g: g[:, None, :]
    x2d = x.reshape(bsz * seq, d)
    for layer in range(depth):
        proj = _inproj(x2d, mod, rows(g_pre_mix), w_all, bias, layer, seq)
        proj3 = proj.reshape(bsz, seq, N_PROJ)
        o_sb = _sb_attention(proj3)
        o_mb = _moba_attention(proj3)
        o_ds = _dsa_attention(proj3)
        x2d = _merge(o_sb, o_mb, o_ds, proj, x2d, mod, rows(g_post_mix), w_sb, w_mb, w_ds, w_o, layer, seq)
        x2d = _ffn(x2d, mod, rows(g_pre_ffn), rows(g_post_ffn), w_up, w_down, layer, seq)
    return x2d.reshape(bsz, seq, d)
```

```python
import functools

import numpy as np
import jax
import jax.numpy as jnp
from jax import lax
from jax.experimental import pallas as pl
from jax.experimental.pallas import tpu as pltpu

F32 = jnp.float32
BF16 = jnp.bfloat16
I32 = jnp.int32

D_MODEL = 1024
HEAD_DIM = 64
H_SB, H_MOBA, H_DSA = 6, 5, 5
W_SB, W_MOBA, W_DSA = H_SB * HEAD_DIM, H_MOBA * HEAD_DIM, H_DSA * HEAD_DIM
MOBA_BLOCK = 256
MOBA_TOPK = 3
DSA_TOPK_MAX = 256
IDX_HEADS = 8
IDX_DIM = 64
D_FF = 2816
N_BRANCH = 3
RMS_EPS = 1e-6
NEG_BIG = -1e30
ALIBI_HEADS = H_MOBA + H_DSA

LANES = 128
PAIR_W = 3 * LANES
QB = 256
KC = 256
INT_MIN = -(2 ** 31)
BF16_ROWS = 16
FINE_BITS = 17

COL_KX = 9 * PAIR_W
COL_QX = COL_KX + LANES
COL_GATE = COL_QX + IDX_HEADS * IDX_DIM
N_PROJ = COL_GATE + N_BRANCH * D_MODEL
PROJ_TN = 1024
FFN_TF = 256

NT = (((1,), (1,)), ((), ()))
TN = (((0,), (0,)), ((), ()))

LOG2E = 1.4426950408889634
_ALIBI = [float(2.0 ** (-8.0 * h / ALIBI_HEADS)) * LOG2E for h in range(1, ALIBI_HEADS + 1)]
SLOPES_MOBA = _ALIBI[0::2]
SLOPES_DSA = _ALIBI[1::2]


def _rms(x):
    return x * lax.rsqrt(jnp.mean(x * x, axis=-1, keepdims=True) + RMS_EPS)


def _fori_by_two(n, body, init):
    def two(i, state):
        return body(2 * i + 1, body(2 * i, state))
    state = lax.fori_loop(0, n // 2, two, init)
    return lax.fori_loop(2 * (n // 2), n, body, state)


def _half_mask(shape, lane_axis, hh):
    lane = lax.broadcasted_iota(I32, shape, lane_axis)
    return (lane < HEAD_DIM) if hh == 0 else (lane >= HEAD_DIM)


def _ada_kernel(c_ref, w_ref, b_ref, o_ref):
    c = c_ref[...]
    ca = c * jax.nn.sigmoid(c)
    o_ref[...] = jnp.dot(ca, w_ref[...], preferred_element_type=F32) + b_ref[...]


def _ada(c, w_ada, b_ada):
    depth, d, n = w_ada.shape
    bsz = c.shape[0]
    tn = 512
    return pl.pallas_call(
        _ada_kernel,
        grid=(depth, n // tn),
        in_specs=[
            pl.BlockSpec((bsz, d), lambda l, j: (0, 0)),
            pl.BlockSpec((None, d, tn), lambda l, j: (l, 0, j)),
            pl.BlockSpec((None, 1, tn), lambda l, j: (l, 0, j)),
        ],
        out_specs=pl.BlockSpec((None, bsz, tn), lambda l, j: (l, 0, j)),
        out_shape=jax.ShapeDtypeStruct((depth, bsz, n), F32),
        name="ada_mod",
    )(c, w_ada, b_ada.reshape(depth, 1, n))


def _inproj_kernel(x_ref, mod_ref, g_ref, w_ref, b_ref, o_ref):
    xh = _rms(x_ref[...])
    h = ((xh * g_ref[...]) * (1.0 + mod_ref[1:2, :]) + mod_ref[0:1, :]).astype(BF16)
    for n0 in range(0, N_PROJ, PROJ_TN):
        cols = slice(n0, n0 + PROJ_TN)
        acc = jnp.dot(h, w_ref[:, cols], preferred_element_type=F32) + b_ref[:, cols]
        if n0 >= COL_GATE:
            acc = jax.nn.sigmoid(acc)
        o_ref[:, cols] = acc.astype(BF16)


def _inproj(x2d, mod, g_pre, w_all, bias, layer, seq):
    t, d = x2d.shape
    tm = 512
    n = w_all.shape[-1]
    per_b = seq // tm
    return pl.pallas_call(
        _inproj_kernel,
        grid=(t // tm,),
        in_specs=[
            pl.BlockSpec((tm, d), lambda i: (i, 0)),
            pl.BlockSpec((None, None, 6, d), lambda i: (layer, i // per_b, 0, 0)),
            pl.BlockSpec((None, 1, d), lambda i: (layer, 0, 0)),
            pl.BlockSpec((None, d, n), lambda i: (layer, 0, 0)),
            pl.BlockSpec((None, 1, n), lambda i: (layer, 0, 0)),
        ],
        out_specs=pl.BlockSpec((tm, n), lambda i: (i, 0)),
        out_shape=jax.ShapeDtypeStruct((t, n), BF16),
        compiler_params=pltpu.CompilerParams(
            dimension_semantics=("parallel",),
            vmem_limit_bytes=56 * 1024 * 1024),
        name="in_proj",
    )(x2d, mod, g_pre, w_all, bias)


def _masked_heads(ref, dst_ref, n_heads):
    for head in range(n_heads):
        grp = ref[:, (head // 2) * LANES:(head // 2 + 1) * LANES]
        dst_ref[head] = jnp.where(_half_mask(grp.shape, 1, head % 2), grp, jnp.zeros_like(grp))


def _sb_kernel(q_ref, k_ref, v_ref, o_ref, qh_ref, lb_ref, l1m_ref, carry_ref, acc_ref):
    qi = pl.program_id(1)
    row = lax.broadcasted_iota(I32, (KC, QB), 0)
    lane = lax.broadcasted_iota(I32, (KC, QB), 1)
    past_diag = row < lane
    upper = (lax.broadcasted_iota(I32, (KC, KC), 1) > lax.broadcasted_iota(I32, (KC, KC), 0)).astype(BF16)

    _masked_heads(q_ref, qh_ref, H_SB)

    def terms(c, carries, diag):
        start = pl.multiple_of(c * KC, KC)
        new = list(carries)
        for p in range(H_SB // 2):
            kc = k_ref[pl.ds(start, KC), p * LANES:(p + 1) * LANES]
            for head in (2 * p, 2 * p + 1):
                z = lax.dot_general(kc, qh_ref[head], NT, preferred_element_type=F32)
                log_beta = jnp.minimum(z, 0.0) - jnp.log2(1.0 + jnp.exp2(-jnp.abs(z)))
                log_1m = log_beta - z
                if diag:
                    log_1m = jnp.where(past_diag, log_1m, 0.0)
                    log_beta = jnp.where(past_diag, log_beta, NEG_BIG)
                lb_ref[head, c] = log_beta
                l1m_ref[head, c] = log_1m.astype(BF16)
                carry_ref[head, c] = carries[head]
                new[head] = carries[head] + _fold8(log_1m, jnp.sum)
        return tuple(new)

    def attend(c):
        start = pl.multiple_of(c * KC, KC)
        for p in range(H_SB // 2):
            vc = v_ref[pl.ds(start, KC), p * LANES:(p + 1) * LANES]
            pv = None
            for head in (2 * p, 2 * p + 1):
                between = jnp.dot(upper, l1m_ref[head, c], preferred_element_type=F32)
                carry = jnp.sum(carry_ref[head, c], axis=0, keepdims=True)
                a = jnp.exp2(lb_ref[head, c] + between + carry)
                vh = jnp.where(_half_mask(vc.shape, 1, head % 2), vc, jnp.zeros_like(vc))
                part = lax.dot_general(vh, a.astype(BF16), TN, preferred_element_type=F32)
                pv = part if pv is None else pv + part
            acc_ref[p] += pv

    def pipelined(i, carries):
        c = qi - 1 - i
        attend(c + 1)
        return terms(c, carries, False)

    acc_ref[...] = jnp.zeros_like(acc_ref)
    carries = terms(qi, tuple(jnp.zeros((8, QB), F32) for _ in range(H_SB)), True)
    _fori_by_two(qi, pipelined, carries)
    attend(0)
    for p in range(H_SB // 2):
        o_ref[p * LANES:(p + 1) * LANES, :] = acc_ref[p].astype(BF16)


def _sb_attention(proj3):
    bsz, seq, _ = proj3.shape
    return pl.pallas_call(
        _sb_kernel,
        grid=(bsz, seq // QB),
        in_specs=[
            pl.BlockSpec((None, QB, PAIR_W), lambda b, i: (b, i, 0)),
            pl.BlockSpec((None, seq, PAIR_W), lambda b, i: (b, 0, 1)),
            pl.BlockSpec((None, seq, PAIR_W), lambda b, i: (b, 0, 2)),
        ],
        out_specs=pl.BlockSpec((None, PAIR_W, QB), lambda b, i: (b, 0, i)),
        out_shape=jax.ShapeDtypeStruct((bsz, PAIR_W, seq), BF16),
        scratch_shapes=[pltpu.VMEM((H_SB, QB, LANES), BF16),
                        pltpu.VMEM((H_SB, seq // KC, KC, QB), F32),
                        pltpu.VMEM((H_SB, seq // KC, KC, QB), BF16),
                        pltpu.VMEM((H_SB, seq // KC, 8, QB), F32),
                        pltpu.VMEM((H_SB // 2, LANES, QB), F32)],
        compiler_params=pltpu.CompilerParams(
            dimension_semantics=("parallel", "arbitrary"),
            vmem_limit_bytes=48 * 1024 * 1024),
        name="sb_attn",
    )(proj3, proj3, proj3)


def _fold8(x, op):
    return op(x.reshape(KC // 8, 8, QB), axis=0)


def _softmax_heads(qh_ref, k_ref, v_ref, s_ref, acc_ref, o_ref, qi, n_heads, table, shared_bias, shift):
    out_row = lax.broadcasted_iota(I32, (LANES, QB), 0)
    n_pairs = (n_heads + 1) // 2
    pair_heads = [[h for h in (2 * p, 2 * p + 1) if h < n_heads] for p in range(n_pairs)]

    def by_head_rows(vals):
        return vals[0] if len(vals) == 1 else jnp.where(out_row < HEAD_DIM, vals[0], vals[1])

    def score(c, maxes, diag):
        start = pl.multiple_of(c * KC, KC)
        shared = None if shared_bias is None else shared_bias(c, diag)
        new = list(maxes)
        for p in range(n_pairs):
            kc = k_ref[pl.ds(start, KC), p * LANES:(p + 1) * LANES]
            for head in pair_heads[p]:
                s = lax.dot_general(kc, qh_ref[head], NT, preferred_element_type=F32) + table(head, diag)
                if shared is not None:
                    s = s + shared
                s_ref[head, c] = s
                new[head] = jnp.maximum(maxes[head], _fold8(s, jnp.max) + shift(c, head))
        return tuple(new)

    maxes = score(qi, tuple(jnp.full((8, QB), NEG_BIG, F32) for _ in range(n_heads)), True)
    maxes = _fori_by_two(qi, lambda c, mx: score(c, mx, False), maxes)
    m = [jnp.max(mx, axis=0, keepdims=True) for mx in maxes]

    acc_ref[...] = jnp.zeros_like(acc_ref)

    def attend(c, sums):
        start = pl.multiple_of(c * KC, KC)
        new = list(sums)
        for p in range(n_pairs):
            vc = v_ref[pl.ds(start, KC), p * LANES:(p + 1) * LANES]
            pv = None
            for head in pair_heads[p]:
                pr = jnp.exp2(s_ref[head, c] + (shift(c, head) - m[head]))
                new[head] = sums[head] + _fold8(pr, jnp.sum)
                vh = vc if len(pair_heads[p]) == 1 else jnp.where(
                    _half_mask(vc.shape, 1, head % 2), vc, jnp.zeros_like(vc))
                part = lax.dot_general(vh, pr.astype(BF16), TN, preferred_element_type=F32)
                pv = part if pv is None else pv + part
            acc_ref[p] += pv
        return tuple(new)

    sums = _fori_by_two(qi + 1, attend, tuple(jnp.zeros((8, QB), F32) for _ in range(n_heads)))
    for p in range(n_pairs):
        denom = by_head_rows([jnp.sum(sums[h], axis=0, keepdims=True) for h in pair_heads[p]])
        o_ref[p * LANES:(p + 1) * LANES, :] = (acc_ref[p] / denom).astype(BF16)


def _alibi_tables(table_ref, slopes, causal):
    row = lax.broadcasted_iota(I32, (KC, QB), 0)
    lane = lax.broadcasted_iota(I32, (KC, QB), 1)
    key_minus_query = (row - lane).astype(F32)
    for head, slope in enumerate(slopes):
        table_ref[head, 0] = slope * key_minus_query
        if causal:
            table_ref[head, 1] = jnp.where(row <= lane, slope * key_minus_query, NEG_BIG)


def _moba_kernel(q_ref, k_ref, v_ref, o_ref, kmean_ref, sel_ref, qh_ref, table_ref, s_ref, acc_ref):
    qi = pl.program_id(1)
    n_blocks = kmean_ref.shape[0]

    @pl.when(qi == 0)
    def _():
        for n in range(n_blocks):
            kb = k_ref[n * MOBA_BLOCK:(n + 1) * MOBA_BLOCK, :].astype(F32)
            kmean_ref[n:n + 1, :] = jnp.mean(kb, axis=0, keepdims=True)
        _alibi_tables(table_ref, SLOPES_MOBA, causal=True)

    blk = lax.broadcasted_iota(I32, (n_blocks, QB), 0)

    _masked_heads(q_ref, qh_ref, H_MOBA)

    for head in range(H_MOBA):
        cols = slice((head // 2) * LANES, (head // 2 + 1) * LANES)
        gate = lax.dot_general(kmean_ref[:, cols], qh_ref[head].astype(F32), NT,
                               preferred_element_type=F32, precision=lax.Precision.HIGHEST)
        rank = jnp.zeros((n_blocks, QB), I32)
        for mb in range(n_blocks):
            gm = gate[mb:mb + 1, :]
            beats = (gm > gate) | ((gm == gate) & (mb < blk))
            rank = rank + jnp.where(beats, (mb < qi).astype(I32), 0)
        selected = ((rank < MOBA_TOPK) & (blk < qi)) | (blk == qi)
        sel_ref[head] = jnp.where(selected, 0.0, NEG_BIG)

    def shift(c, head):
        return sel_ref[head, pl.ds(c, 1), :] + SLOPES_MOBA[head] * ((c - qi) * KC).astype(F32)

    _softmax_heads(qh_ref, k_ref, v_ref, s_ref, acc_ref, o_ref, qi, H_MOBA,
                   table=lambda head, diag: table_ref[head, 1 if diag else 0],
                   shared_bias=None, shift=shift)


def _moba_attention(proj3):
    bsz, seq, _ = proj3.shape
    n_blocks = seq // MOBA_BLOCK
    return pl.pallas_call(
        _moba_kernel,
        grid=(bsz, seq // QB),
        in_specs=[
            pl.BlockSpec((None, QB, PAIR_W), lambda b, i: (b, i, 3)),
            pl.BlockSpec((None, seq, PAIR_W), lambda b, i: (b, 0, 4)),
            pl.BlockSpec((None, seq, PAIR_W), lambda b, i: (b, 0, 5)),
        ],
        out_specs=pl.BlockSpec((None, PAIR_W, QB), lambda b, i: (b, 0, i)),
        out_shape=jax.ShapeDtypeStruct((bsz, PAIR_W, seq), BF16),
        scratch_shapes=[pltpu.VMEM((n_blocks, PAIR_W), F32),
                        pltpu.VMEM((H_MOBA, n_blocks, QB), F32),
                        pltpu.VMEM((H_MOBA, QB, LANES), BF16),
                        pltpu.VMEM((H_MOBA, 2, KC, QB), F32),
                        pltpu.VMEM((H_MOBA, seq // KC, KC, QB), F32),
                        pltpu.VMEM((PAIR_W // LANES, LANES, QB), F32)],
        compiler_params=pltpu.CompilerParams(
            dimension_semantics=("parallel", "arbitrary"),
            vmem_limit_bytes=48 * 1024 * 1024),
        name="moba_attn",
    )(proj3, proj3, proj3)


def _dsa_kernel(q_ref, k_ref, v_ref, qx_ref, kxq_ref, kx_ref, o_ref,
                kx2_ref, score_ref, coarse_ref, tau_ref, cidx_ref, qh_ref, table_ref, s_ref, acc_ref, *, top):
    qi = pl.program_id(1)
    seq = k_ref.shape[0]

    @pl.when(qi == 0)
    def _():
        kx = kx_ref[...].astype(F32)
        dup = jnp.where(_half_mask(kx.shape, 1, 0), kx, pltpu.roll(kx, HEAD_DIM, 1))
        kx2_ref[...] = dup.astype(BF16)
        _alibi_tables(table_ref, SLOPES_DSA, causal=False)

    row = lax.broadcasted_iota(I32, (KC, QB), 0)
    lane = lax.broadcasted_iota(I32, (KC, QB), 1)
    causal_diag = row <= lane

    pick = (lax.broadcasted_iota(I32, (IDX_HEADS, LANES), 1)
            == lax.broadcasted_iota(I32, (IDX_HEADS, LANES), 0) + IDX_DIM).astype(BF16)
    w_t = lax.dot_general(pick, kxq_ref[...], NT, preferred_element_type=F32)

    def score_rows(c, size, diag):
        start = pl.multiple_of(c * KC, KC)
        kc = kx2_ref[pl.ds(start, size), :]
        sc = jnp.zeros((size, QB), F32)
        for pp in range(IDX_HEADS // 2):
            qp = qx_ref[:, pp * LANES:(pp + 1) * LANES]
            for hh in range(2):
                h = 2 * pp + hh
                qh = jnp.where(_half_mask(qp.shape, 1, hh), qp, jnp.zeros_like(qp))
                lg = lax.dot_general(kc, qh, NT, preferred_element_type=F32)
                sc = sc + jnp.maximum(lg, 0.0) * w_t[h:h + 1, :]
        if diag:
            sc = jnp.where(causal_diag, sc, NEG_BIG)
        score_ref[pl.ds(start, size), :] = sc
        coarse_ref[pl.ds(start, size), :] = sc.astype(BF16)

    score_rows(qi, KC, True)

    def _score_two(i, carry):
        score_rows(2 * i, 2 * KC, False)
        return carry

    def _score_one(c, carry):
        score_rows(c, KC, False)
        return carry

    lax.fori_loop(0, qi // 2, _score_two, 0)
    lax.fori_loop(2 * (qi // 2), qi, _score_one, 0)

    def count(n, pred):
        accs = [jnp.zeros((8, QB), I32) for _ in range(4)]
        for c in range(n):
            hit = jnp.where(pred(score_ref[c * KC:(c + 1) * KC, :], c), 1, 0).reshape(KC // 8, 8, QB)
            for r in range(KC // 8):
                accs[r % 4] = accs[r % 4] + hit[r]
        return jnp.sum((accs[0] + accs[1]) + (accs[2] + accs[3]), axis=0, keepdims=True)

    def ordered_to_float(key):
        return pltpu.bitcast(jnp.where(key < 0, key ^ jnp.int32(0x7FFFFFFF), key), F32)

    def count_coarse(n, cand):
        accs = [jnp.zeros((BF16_ROWS, QB), jnp.int16) for _ in range(4)]
        for c in range(n):
            tile = coarse_ref[c * KC:(c + 1) * KC, :].reshape(KC // BF16_ROWS, BF16_ROWS, QB)
            hit = jnp.where(tile >= cand, jnp.int16(1), jnp.int16(0))
            for r in range(KC // BF16_ROWS):
                accs[r % 4] = accs[r % 4] + hit[r]
        total = (accs[0] + accs[1]) + (accs[2] + accs[3])
        return jnp.sum(total.astype(I32), axis=0, keepdims=True)

    def find_threshold(n):
        def coarse_step(i, k):
            cand_k = k + lax.shift_left(jnp.int32(1), 15 - i)
            pattern = jnp.where(cand_k < 0, cand_k ^ jnp.int32(0x7FFF), cand_k)
            cand = pltpu.bitcast(lax.shift_left(pattern, 16), F32).astype(BF16)
            cnt = count_coarse(n, jnp.broadcast_to(cand, (BF16_ROWS, QB))[None])
            return jnp.where(cnt >= top, cand_k, k)

        k16 = lax.fori_loop(0, 16, coarse_step, jnp.full((1, QB), -(2 ** 15), I32))

        def fine_step(i, state):
            key, cnt_key = state
            cand_key = key + lax.shift_left(jnp.int32(1), FINE_BITS - 1 - i)
            cand = ordered_to_float(cand_key)
            cnt = count(n, lambda s, c: s >= cand)
            ok = cnt >= top
            return jnp.where(ok, cand_key, key), jnp.where(ok, cnt, cnt_key)

        base = lax.shift_left(k16, 16)
        start = (jnp.where(k16 >= 0, base - 0x8000, base + 0x7FFF), jnp.full((1, QB), n * KC, I32))
        key, at_least = lax.fori_loop(0, FINE_BITS, fine_step, start)
        tau = ordered_to_float(key)
        tau_ref[...] = jnp.broadcast_to(tau, tau_ref.shape)

        @pl.when(jnp.max(at_least) > top)
        def _():
            need = top - count(n, lambda s, c: s > tau)

            def idx_step(i, x):
                cand = x + lax.shift_left(jnp.int32(1), 11 - i)
                cnt = count(n, lambda s, c: (s == tau) & ((c * KC + row) < cand))
                return jnp.where(cnt < need, cand, x)
            x = lax.fori_loop(0, 12, idx_step, jnp.zeros((1, QB), I32))
            cidx_ref[...] = jnp.broadcast_to(x, cidx_ref.shape)

    tau_ref[...] = jnp.full(tau_ref.shape, -jnp.inf, F32)
    cidx_ref[...] = jnp.full(cidx_ref.shape, seq, I32)
    for n in range(2, seq // KC + 1):
        pl.when(qi == n - 1)(functools.partial(find_threshold, n))

    tau = tau_ref[0:1, :]
    cidx = cidx_ref[0:1, :]

    def keep_bias(c, diag):
        scores = score_ref[pl.ds(pl.multiple_of(c * KC, KC), KC), :]
        keep = (scores > tau) | ((scores == tau) & ((c * KC + row) <= cidx))
        if diag:
            keep = keep & causal_diag
        return jnp.where(keep, 0.0, NEG_BIG)

    _masked_heads(q_ref, qh_ref, H_DSA)

    _softmax_heads(qh_ref, k_ref, v_ref, s_ref, acc_ref, o_ref, qi, H_DSA,
                   table=lambda head, diag: table_ref[head, 0],
                   shared_bias=keep_bias,
                   shift=lambda c, head: SLOPES_DSA[head] * ((c - qi) * KC).astype(F32))


def _dsa_attention(proj3):
    bsz, seq, _ = proj3.shape
    top = min(DSA_TOPK_MAX, seq // 4)
    assert top == QB, "the first query block must keep every admissible key"
    return pl.pallas_call(
        functools.partial(_dsa_kernel, top=top),
        grid=(bsz, seq // QB),
        in_specs=[
            pl.BlockSpec((None, QB, PAIR_W), lambda b, i: (b, i, 6)),
            pl.BlockSpec((None, seq, PAIR_W), lambda b, i: (b, 0, 7)),
            pl.BlockSpec((None, seq, PAIR_W), lambda b, i: (b, 0, 8)),
            pl.BlockSpec((None, QB, IDX_HEADS * IDX_DIM), lambda b, i: (b, i, COL_QX // (IDX_HEADS * IDX_DIM))),
            pl.BlockSpec((None, QB, LANES), lambda b, i: (b, i, COL_KX // LANES)),
            pl.BlockSpec((None, seq, LANES), lambda b, i: (b, 0, COL_KX // LANES)),
        ],
        out_specs=pl.BlockSpec((None, PAIR_W, QB), lambda b, i: (b, 0, i)),
        out_shape=jax.ShapeDtypeStruct((bsz, PAIR_W, seq), BF16),
        scratch_shapes=[pltpu.VMEM((seq, LANES), BF16),
                        pltpu.VMEM((seq, QB), F32),
                        pltpu.VMEM((seq, QB), BF16),
                        pltpu.VMEM((8, QB), F32),
                        pltpu.VMEM((8, QB), I32),
                        pltpu.VMEM((H_DSA, QB, LANES), BF16),
                        pltpu.VMEM((H_DSA, 1, KC, QB), F32),
                        pltpu.VMEM((H_DSA, seq // KC, KC, QB), F32),
                        pltpu.VMEM((PAIR_W // LANES, LANES, QB), F32)],
        compiler_params=pltpu.CompilerParams(
            dimension_semantics=("parallel", "arbitrary"),
            vmem_limit_bytes=48 * 1024 * 1024),
        name="dsa_attn",
    )(proj3, proj3, proj3, proj3, proj3, proj3)


def _merge_kernel(osb_ref, omb_ref, ods_ref, gsb_ref, gmb_ref, gds_ref, x_ref, mod_ref, gpost_ref,
                  wsb_ref, wmb_ref, wds_ref, wo_ref, out_ref):
    def branch(o_ref, w_ref, g_ref):
        y = lax.dot_general(o_ref[...], w_ref[...], TN, preferred_element_type=F32)
        return g_ref[...].astype(F32) * y

    merged = branch(osb_ref, wsb_ref, gsb_ref) + branch(omb_ref, wmb_ref, gmb_ref) + branch(ods_ref, wds_ref, gds_ref)
    y = jnp.dot(merged.astype(BF16), wo_ref[...], preferred_element_type=F32)
    out_ref[...] = x_ref[...] + mod_ref[2:3, :] * (_rms(y) * gpost_ref[...])


def _merge(o_sb, o_mb, o_ds, proj, x2d, mod, g_post, w_sb, w_mb, w_ds, w_o, layer, seq):
    t, d = x2d.shape
    tm = 1024
    per_b = seq // tm
    gate_blk = COL_GATE // d
    o_spec = pl.BlockSpec((None, PAIR_W, tm), lambda i: (i // per_b, 0, i % per_b))
    w_spec = pl.BlockSpec((None, PAIR_W, d), lambda i: (layer, 0, 0))
    return pl.pallas_call(
        _merge_kernel,
        grid=(t // tm,),
        in_specs=[
            o_spec, o_spec, o_spec,
            pl.BlockSpec((tm, d), lambda i: (i, gate_blk)),
            pl.BlockSpec((tm, d), lambda i: (i, gate_blk + 1)),
            pl.BlockSpec((tm, d), lambda i: (i, gate_blk + 2)),
            pl.BlockSpec((tm, d), lambda i: (i, 0)),
            pl.BlockSpec((None, None, 6, d), lambda i: (layer, i // per_b, 0, 0)),
            pl.BlockSpec((None, 1, d), lambda i: (layer, 0, 0)),
            w_spec, w_spec, w_spec,
            pl.BlockSpec((None, d, d), lambda i: (layer, 0, 0)),
        ],
        out_specs=pl.BlockSpec((tm, d), lambda i: (i, 0)),
        out_shape=jax.ShapeDtypeStruct((t, d), F32),
        compiler_params=pltpu.CompilerParams(
            dimension_semantics=("parallel",),
            vmem_limit_bytes=48 * 1024 * 1024),
        name="merge_out",
    )(o_sb, o_mb, o_ds, proj, proj, proj, x2d, mod, g_post, w_sb, w_mb, w_ds, w_o)


def _ffn_kernel(x_ref, mod_ref, gpre_ref, gpost_ref, wg_ref, wu_ref, wd_ref, out_ref, h_ref, acc_ref):
    xh = _rms(x_ref[...])
    h_ref[...] = ((xh * gpre_ref[...]) * (1.0 + mod_ref[4:5, :]) + mod_ref[3:4, :]).astype(BF16)
    for f in range(0, D_FF, FFN_TF):
        h = h_ref[...]
        gate = jnp.dot(h, wg_ref[:, f:f + FFN_TF], preferred_element_type=F32)
        up = jnp.dot(h, wu_ref[:, f:f + FFN_TF], preferred_element_type=F32)
        act = ((gate * jax.nn.sigmoid(gate)) * up).astype(BF16)
        part = jnp.dot(act, wd_ref[f:f + FFN_TF, :], preferred_element_type=F32)
        if f == 0:
            acc_ref[...] = part
        else:
            acc_ref[...] += part
    y = acc_ref[...]
    out_ref[...] = x_ref[...] + mod_ref[5:6, :] * (_rms(y) * gpost_ref[...])


def _ffn(x2d, mod, g_pre, g_post, w_up, w_down, layer, seq):
    t, d = x2d.shape
    tm = 512
    per_b = seq // tm
    return pl.pallas_call(
        _ffn_kernel,
        grid=(t // tm,),
        in_specs=[
            pl.BlockSpec((tm, d), lambda i: (i, 0)),
            pl.BlockSpec((None, None, 6, d), lambda i: (layer, i // per_b, 0, 0)),
            pl.BlockSpec((None, 1, d), lambda i: (layer, 0, 0)),
            pl.BlockSpec((None, 1, d), lambda i: (layer, 0, 0)),
            pl.BlockSpec((None, d, D_FF), lambda i: (layer, 0, 0)),
            pl.BlockSpec((None, d, D_FF), lambda i: (layer, 0, 1)),
            pl.BlockSpec((None, D_FF, d), lambda i: (layer, 0, 0)),
        ],
        out_specs=pl.BlockSpec((tm, d), lambda i: (i, 0)),
        out_shape=jax.ShapeDtypeStruct((t, d), F32),
        scratch_shapes=[pltpu.VMEM((tm, d), BF16), pltpu.VMEM((tm, d), F32)],
        compiler_params=pltpu.CompilerParams(
            dimension_semantics=("parallel",),
            vmem_limit_bytes=56 * 1024 * 1024),
        name="ffn",
    )(x2d, mod, g_pre, g_post, w_up, w_up, w_down)


def _pad_rows(w, n):
    return jnp.pad(w, ((0, 0), (0, n - w.shape[1]), (0, 0)))


def _w_in_segments():
    sizes = [W_SB] * 3 + [W_MOBA] * 3 + [W_DSA] * 3 + [IDX_HEADS * IDX_DIM, IDX_DIM + IDX_HEADS, N_BRANCH * D_MODEL]
    src = [int(v) for v in np.cumsum([0] + sizes[:-1])]
    q_scale = HEAD_DIM ** -0.5 * LOG2E
    dst = [j * PAIR_W for j in range(9)] + [COL_QX, COL_KX, COL_GATE]
    scales = [q_scale, 1.0, 1.0] * 3 + [1.0, 1.0, 1.0]
    return list(zip(src, sizes, dst, scales))


def _w_layout_kernel(w_ref, o_ref):
    o_ref[...] = jnp.zeros_like(o_ref)
    for src, width, dst, scale in _w_in_segments():
        o_ref[:, dst:dst + width] = (w_ref[:, src:src + width] * scale).astype(BF16)


def _layout_w_in(w_in, b_gate):
    depth, d, d_in = w_in.shape
    rows = 128
    w_all = pl.pallas_call(
        _w_layout_kernel,
        grid=(depth, d // rows),
        in_specs=[pl.BlockSpec((None, rows, d_in), lambda l, i: (l, i, 0))],
        out_specs=pl.BlockSpec((None, rows, N_PROJ), lambda l, i: (l, i, 0)),
        out_shape=jax.ShapeDtypeStruct((depth, d, N_PROJ), BF16),
        name="w_layout",
    )(w_in)
    bias = jnp.concatenate([jnp.zeros((depth, COL_GATE), F32), b_gate], axis=-1)[:, None, :]
    return w_all, bias


def kernel(x, c, w_ada, b_ada, g_pre_mix, g_post_mix, w_in, b_gate, w_proj_sb, w_proj_moba,
           w_proj_dsa, w_o, g_pre_ffn, g_post_ffn, w_up, w_down):
    bsz, seq, d = x.shape
    depth = w_ada.shape[0]
    assert d == D_MODEL and seq % QB == 0 and QB == MOBA_BLOCK and KC == QB
    mod = _ada(c, w_ada, b_ada).reshape(depth, bsz, 6, d)
    w_all, bias = _layout_w_in(w_in, b_gate)
    assert w_all.shape[-1] == N_PROJ
    w_sb = w_proj_sb.astype(BF16)
    w_mb = _pad_rows(w_proj_moba, PAIR_W).astype(BF16)
    w_ds = _pad_rows(w_proj_dsa, PAIR_W).astype(BF16)
    w_o, w_up, w_down = w_o.astype(BF16), w_up.astype(BF16), w_down.astype(BF16)
    rows = lambda g: g[:, None, :]
    x2d = x.reshape(bsz * seq, d)
    for layer in range(depth):
        proj = _inproj(x2d, mod, rows(g_pre_mix), w_all, bias, layer, seq)
        proj3 = proj.reshape(bsz, seq, N_PROJ)
        o_sb = _sb_attention(proj3)
        o_mb = _moba_attention(proj3)
        o_ds = _dsa_attention(proj3)
        x2d = _merge(o_sb, o_mb, o_ds, proj, x2d, mod, rows(g_post_mix), w_sb, w_mb, w_ds, w_o, layer, seq)
        x2d = _ffn(x2d, mod, rows(g_pre_ffn), rows(g_post_ffn), w_up, w_down, layer, seq)
    return x2d.reshape(bsz, seq, d)
```
